```python
import jax, jax.numpy as jnp
from jax import lax
import numpy as np

D_MODEL = 1024
BATCH = 4
SEQ = 8192
DEPTH = 4

CHUNK = 64
EPS = 1e-6

A_HEADS = 4
A_WIDTH = D_MODEL
A_HEAD_DIM = A_WIDTH // A_HEADS
CONV_WIDTH = 4

B_HEADS = 8
B_WIDTH = D_MODEL
B_HEAD_DIM = B_WIDTH // B_HEADS
SB_BLOCK = 128

POOL_WINDOWS = (2, 4, 8, 16)
C_GROUPS = len(POOL_WINDOWS)
C_WIDTH = 2 * D_MODEL
C_GROUP_DIM = C_WIDTH // C_GROUPS

N_EVEN = (DEPTH + 1) // 2
N_ODD = DEPTH // 2
EVEN_SPLITS = [A_WIDTH] * 5 + [A_HEADS] * 2 + [B_WIDTH] * 4
EVEN_IN = sum(EVEN_SPLITS)
EVEN_OUT = A_WIDTH + B_WIDTH
ODD_IN = 2 * C_WIDTH

kernel_name = "hybrid_mlstm_stickbreak_pool_trunk"


def rmsnorm(x, g):
    xf = x.astype(jnp.float32)
    y = xf * lax.rsqrt(jnp.mean(xf * xf, axis=-1, keepdims=True) + EPS)
    return y * g.astype(jnp.float32)


def causal_dwconv(x, w):
    K = w.shape[0]
    S = x.shape[1]
    xp = jnp.pad(x, ((0, 0), (K - 1, 0), (0, 0)))
    return sum(xp[:, k:k + S] * w[k].astype(jnp.float32) for k in range(K))


def split_cols(u, sizes):
    idx = list(np.cumsum(sizes)[:-1])
    return jnp.split(u, idx, axis=-1)


def mlstm(q, k, v, ig, lf):
    Bn, S, H, dh = q.shape
    nc = S // CHUNK
    k = k * (dh ** -0.5)

    def chunks4(t):
        return t.reshape(Bn, nc, CHUNK, H, dh).transpose(1, 0, 3, 2, 4)

    def chunks3(t):
        return t.reshape(Bn, nc, CHUNK, H).transpose(1, 0, 3, 2)

    xs = (chunks4(q), chunks4(k), chunks4(v), chunks3(ig), chunks3(lf))
    tri = jnp.tril(jnp.ones((CHUNK, CHUNK), dtype=bool))

    def step(carry, inp):
        C, n, m = carry
        qc, kc, vc, ic, fc = inp
        b = jnp.cumsum(fc, axis=-1)
        Dm = jnp.where(tri, b[..., :, None] - b[..., None, :] + ic[..., None, :], -1e30)
        m_inter = b + m[..., None]
        m_t = jnp.maximum(m_inter, jnp.max(Dm, axis=-1))
        W = jnp.exp(Dm - m_t[..., None])
        Sc = jnp.einsum('bhtd,bhsd->bhts', qc, kc) * W
        s_inter = jnp.exp(m_inter - m_t)
        num = (jnp.einsum('bhts,bhsd->bhtd', Sc, vc)
               + s_inter[..., None] * jnp.einsum('bhvk,bhtk->bhtv', C, qc))
        den = jnp.sum(Sc, axis=-1) + s_inter * jnp.einsum('bhk,bhtk->bht', n, qc)
        h = num / jnp.maximum(jnp.abs(den), jnp.exp(-m_t))[..., None]
        g = b[..., -1]
        a = g[..., None] - b + ic
        m_new = jnp.maximum(g + m, jnp.max(a, axis=-1))
        decay = jnp.exp(g + m - m_new)
        wa = jnp.exp(a - m_new[..., None])
        C_new = decay[..., None, None] * C + jnp.einsum('bhs,bhsv,bhsk->bhvk', wa, vc, kc)
        n_new = decay[..., None] * n + jnp.einsum('bhs,bhsk->bhk', wa, kc)
        return (C_new, n_new, m_new), h

    init = (jnp.zeros((Bn, H, dh, dh), jnp.float32),
            jnp.zeros((Bn, H, dh), jnp.float32),
            jnp.zeros((Bn, H), jnp.float32))
    _, hs = lax.scan(step, init, xs)
    return hs.transpose(1, 0, 3, 2, 4).reshape(Bn, S, H * dh)


def stick_breaking(q, k, v):
    S, dh = q.shape[2], q.shape[3]
    scale = dh ** -0.5
    outs = []
    for blk in range(S // SB_BLOCK):
        q0 = blk * SB_BLOCK
        q1 = q0 + SB_BLOCK
        z = jnp.einsum('bhqd,bhkd->bhqk', q[:, :, q0:q1], k[:, :, :q1]) * scale
        mask = jnp.arange(q1)[None, :] < (q0 + jnp.arange(SB_BLOCK))[:, None]
        log_1mb = jnp.where(mask, jax.nn.log_sigmoid(-z), 0.0)
        after = lax.cumsum(log_1mb, axis=3, reverse=True) - log_1mb
        A = jnp.where(mask, jnp.exp(jax.nn.log_sigmoid(z) + after), 0.0)
        outs.append(jnp.einsum('bhqk,bhkd->bhqd', A, v[:, :, :q1]))
    return jnp.concatenate(outs, axis=2)


def multiscale_pool(p):
    S = p.shape[1]
    cs = jnp.cumsum(p, axis=1)
    cs0 = jnp.pad(cs, ((0, 0), (1, 0), (0, 0), (0, 0)))
    t = jnp.arange(S)
    outs = []
    for g, w in enumerate(POOL_WINDOWS):
        upper = cs0[:, 1:, g]
        lower = jnp.pad(cs0[:, :S + 1 - w, g], ((0, 0), (w - 1, 0), (0, 0)))
        cnt = jnp.minimum(t + 1, w).astype(jnp.float32)
        outs.append((upper - lower) / cnt[None, :, None] - p[:, :, g])
    return jnp.stack(outs, axis=2)


def head_norm(h, g, n_heads):
    Bn, S, W = h.shape
    hh = h.reshape(Bn, S, n_heads, W // n_heads)
    hh = hh * lax.rsqrt(jnp.mean(hh * hh, axis=-1, keepdims=True) + EPS)
    return hh.reshape(Bn, S, W) * g.astype(jnp.float32)


def even_layer(x, pre_g, post_g, w_in, conv_w, b_i, b_f, hn_g, w_out):
    Bn, S, _ = x.shape
    h = rmsnorm(x, pre_g)
    u = h @ w_in.astype(jnp.float32)
    qa, ka, va, oa, za, ia, fa, qb, kb, vb, zb = split_cols(u, EVEN_SPLITS)
    qk = jax.nn.silu(causal_dwconv(jnp.concatenate([qa, ka], axis=-1), conv_w))
    qa, ka = qk[..., :A_WIDTH], qk[..., A_WIDTH:]
    ig = ia + b_i.astype(jnp.float32)
    lf = jax.nn.log_sigmoid(fa + b_f.astype(jnp.float32))
    hs = (Bn, S, A_HEADS, A_HEAD_DIM)
    ha = mlstm(qa.reshape(hs), ka.reshape(hs), va.reshape(hs), ig, lf)
    ha = jax.nn.sigmoid(oa) * ha
    ya = head_norm(ha, hn_g, A_HEADS) * jax.nn.silu(za)
    def heads(t):
        return t.reshape(Bn, S, B_HEADS, B_HEAD_DIM).transpose(0, 2, 1, 3)
    hb = stick_breaking(heads(qb), heads(kb), heads(vb))
    hb = hb.transpose(0, 2, 1, 3).reshape(Bn, S, B_WIDTH)
    yb = hb * jax.nn.silu(zb)
    y = jnp.concatenate([ya, yb], axis=-1) @ w_out.astype(jnp.float32)
    return x + rmsnorm(y, post_g).astype(x.dtype)


def odd_layer(x, pre_g, post_g, w_in, pool_w, pool_scale, w_out):
    Bn, S, _ = x.shape
    h = rmsnorm(x, pre_g)
    u = h @ w_in.astype(jnp.float32)
    p, z = u[..., :C_WIDTH], u[..., C_WIDTH:]
    pooled = multiscale_pool(p.reshape(Bn, S, C_GROUPS, C_GROUP_DIM))
    mixed = jnp.einsum('bsgc,gce->bsge', pooled, pool_w.astype(jnp.float32))
    mixed = mixed.reshape(Bn, S, C_WIDTH) * pool_scale.astype(jnp.float32)
    y = (mixed * jax.nn.silu(z)) @ w_out.astype(jnp.float32)
    return x + rmsnorm(y, post_g).astype(x.dtype)


def setup_inputs(seed: int = 0) -> dict:
    key = jax.random.key(seed)
    ks = jax.random.split(key, 14)
    f32 = jnp.float32
    x = jax.random.normal(ks[0], (BATCH, SEQ, D_MODEL), f32)
    pre_norm_g = 1.0 + 0.02 * jax.random.normal(ks[1], (DEPTH, D_MODEL), f32)
    post_norm_g = 1.0 + 0.02 * jax.random.normal(ks[2], (DEPTH, D_MODEL), f32)
    w_in_ab = jax.random.normal(ks[3], (N_EVEN, D_MODEL, EVEN_IN), f32) * D_MODEL ** -0.5
    conv_qk = jax.random.normal(ks[4], (N_EVEN, CONV_WIDTH, 2 * A_WIDTH), f32) * CONV_WIDTH ** -0.5
    bias_i = 0.1 * jax.random.normal(ks[5], (N_EVEN, A_HEADS), f32)
    bias_f = 3.0 + 3.0 * jax.random.uniform(ks[6], (N_EVEN, A_HEADS), f32)
    head_norm_g = 1.0 + 0.02 * jax.random.normal(ks[7], (N_EVEN, A_WIDTH), f32)
    w_out_ab = jax.random.normal(ks[8], (N_EVEN, EVEN_OUT, D_MODEL), f32) * EVEN_OUT ** -0.5
    w_in_c = jax.random.normal(ks[9], (N_ODD, D_MODEL, ODD_IN), f32) * D_MODEL ** -0.5
    pool_w = jax.random.normal(ks[10], (N_ODD, C_GROUPS, C_GROUP_DIM, C_GROUP_DIM), f32) * C_GROUP_DIM ** -0.5
    pool_scale = 1.0 + 0.1 * jax.random.normal(ks[11], (N_ODD, C_WIDTH), f32)
    w_out_c = jax.random.normal(ks[12], (N_ODD, C_WIDTH, D_MODEL), f32) * C_WIDTH ** -0.5
    return {"x": x, "pre_norm_g": pre_norm_g, "post_norm_g": post_norm_g,
            "w_in_ab": w_in_ab, "conv_qk": conv_qk, "bias_i": bias_i, "bias_f": bias_f,
            "head_norm_g": head_norm_g, "w_out_ab": w_out_ab,
            "w_in_c": w_in_c, "pool_w": pool_w, "pool_scale": pool_scale, "w_out_c": w_out_c}


def reference(x, pre_norm_g, post_norm_g, w_in_ab, conv_qk, bias_i, bias_f, head_norm_g,
              w_out_ab, w_in_c, pool_w, pool_scale, w_out_c):
    for layer in range(DEPTH):
        if layer % 2 == 0:
            e = layer // 2
            x = even_layer(x, pre_norm_g[layer], post_norm_g[layer], w_in_ab[e], conv_qk[e],
                           bias_i[e], bias_f[e], head_norm_g[e], w_out_ab[e])
        else:
            o = layer // 2
            x = odd_layer(x, pre_norm_g[layer], post_norm_g[layer], w_in_c[o], pool_w[o],
                          pool_scale[o], w_out_c[o])
    return x
```

```python
import functools

import jax
import jax.numpy as jnp
from jax import lax
from jax.experimental import pallas as pl
from jax.experimental.pallas import tpu as pltpu

F32 = jnp.float32
BF16 = jnp.bfloat16
EPS = 1e-6

A_HEADS = 4
B_HEADS = 8
CONV_WIDTH = 4
POOL_WINDOWS = (2, 4, 8, 16)
LANES = 128
GATE_ROWS = 16
POOL_HALO = 16
CONV_HALO = 8
SB_SKIP_LOG = -104.0
VMEM_LIMIT = 56 * 1024 * 1024

NT_DIMS = (((1,), (1,)), ((), ()))
TN_DIMS = (((0,), (0,)), ((), ()))


def _sigmoid(x):
    return 1.0 / (1.0 + jnp.exp(-x))


def _silu(x):
    return x * _sigmoid(x)


def _softplus(x):
    return jnp.maximum(x, 0.0) + jnp.log1p(jnp.exp(-jnp.abs(x)))


def _log_sigmoid(x):
    return -_softplus(-x)


def _split3(x):
    hi = x.astype(BF16)
    r1 = x - hi.astype(F32)
    mid = r1.astype(BF16)
    lo = (r1 - mid.astype(F32)).astype(BF16)
    return hi, mid, lo


def _dot(a, b):
    return jnp.dot(a, b, preferred_element_type=F32)


def _in_proj_kernel(x_ref, g_ref, w_ref, wg_ref, wgt_ref, u_ref, gc_ref, gr_ref, h_ref):
    @pl.when(pl.program_id(1) == 0)
    def _():
        x = x_ref[...]
        ms = jnp.mean(x * x, axis=-1, keepdims=True)
        h = (x * lax.rsqrt(ms + EPS) * g_ref[...]).astype(BF16)
        h_ref[...] = h
        gc_ref[...] = _dot(h, wg_ref[...])
        gr_ref[...] = lax.dot_general(wgt_ref[...], h, NT_DIMS,
                                      preferred_element_type=F32)

    u_ref[...] = _dot(h_ref[...], w_ref[...]).astype(BF16)


def _in_proj(x, g, w_main, w_gate, w_gate_t, tm, tn):
    n, d = x.shape
    cols = w_main.shape[1]
    return pl.pallas_call(
        _in_proj_kernel,
        grid=(n // tm, cols // tn),
        in_specs=[
            pl.BlockSpec((tm, d), lambda i, j: (i, 0)),
            pl.BlockSpec((1, d), lambda i, j: (0, 0)),
            pl.BlockSpec((d, tn), lambda i, j: (0, j)),
            pl.BlockSpec((d, LANES), lambda i, j: (0, 0)),
            pl.BlockSpec((GATE_ROWS, d), lambda i, j: (0, 0)),
        ],
        out_specs=[
            pl.BlockSpec((tm, tn), lambda i, j: (i, j)),
            pl.BlockSpec((tm, LANES), lambda i, j: (i, 0)),
            pl.BlockSpec((GATE_ROWS, tm), lambda i, j: (0, i)),
        ],
        out_shape=[
            jax.ShapeDtypeStruct((n, cols), BF16),
            jax.ShapeDtypeStruct((n, LANES), F32),
            jax.ShapeDtypeStruct((GATE_ROWS, n), F32),
        ],
        scratch_shapes=[pltpu.VMEM((tm, d), BF16)],
        compiler_params=pltpu.CompilerParams(
            dimension_semantics=("arbitrary", "arbitrary"), vmem_limit_bytes=VMEM_LIMIT),
        name="even_in_proj",
    )(x, g, w_main, w_gate, w_gate_t)


def _causal_conv(x, tail, w):
    rows = x.shape[0]
    ext = jnp.concatenate([tail, x], axis=0)
    y = x * w[CONV_WIDTH - 1:CONV_WIDTH, :]
    for k in range(1, CONV_WIDTH):
        shifted = pltpu.roll(ext, k, axis=0)[CONV_HALO:CONV_HALO + rows, :]
        y = y + shifted * w[CONV_WIDTH - 1 - k:CONV_WIDTH - k, :]
    return y


def _mlstm_kernel(q_ref, k_ref, v_ref, o_ref, z_ref, gc_ref, gr_ref, cw_ref, brow_ref, bcol_ref,
                  hng_ref, tril_ref, triu_ref, y_ref, ct_ref, m_ref, qt_ref, kt_ref, *, heads, dh):
    chunk = q_ref.shape[0]
    width = heads * dh

    @pl.when(pl.program_id(1) == 0)
    def _():
        ct_ref[...] = jnp.zeros_like(ct_ref)
        m_ref[...] = jnp.zeros_like(m_ref)
        qt_ref[...] = jnp.zeros_like(qt_ref)
        kt_ref[...] = jnp.zeros_like(kt_ref)

    gact_c = gc_ref[...] + brow_ref[...]
    gact_r = gr_ref[...] + bcol_ref[...]
    tril = tril_ref[...]
    triu = triu_ref[...]
    b_c_all = sum(_dot(tril, p) for p in _split3(_log_sigmoid(gact_c)))
    b_r_all = sum(_dot(p, triu) for p in _split3(_log_sigmoid(gact_r)))

    t_idx = lax.broadcasted_iota(jnp.int32, (chunk, chunk), 0)
    s_idx = lax.broadcasted_iota(jnp.int32, (chunk, chunk), 1)
    causal = s_idx <= t_idx
    ones_blk = jnp.where(lax.broadcasted_iota(jnp.int32, (chunk, LANES), 1) == 0, 1.0, 0.0).astype(BF16)

    for h in range(heads):
        sl = slice(h * dh, (h + 1) * dh)
        qpre = q_ref[:, sl].astype(F32)
        kpre = k_ref[:, sl].astype(F32)
        q = _silu(_causal_conv(qpre, qt_ref[:, sl], cw_ref[:, sl]))
        k = _silu(_causal_conv(kpre, kt_ref[:, sl], cw_ref[:, width + h * dh:width + (h + 1) * dh]))
        k = k * (dh ** -0.5)
        qt_ref[:, sl] = qpre[chunk - CONV_HALO:, :]
        kt_ref[:, sl] = kpre[chunk - CONV_HALO:, :]
        qb = q.astype(BF16)
        kb = k.astype(BF16)
        v_aug = jnp.concatenate([v_ref[:, sl], ones_blk], axis=1)

        b_c = b_c_all[:, heads + h:heads + h + 1]
        ig_c = gact_c[:, h:h + 1]
        b_r = b_r_all[heads + h:heads + h + 1, :]
        ig_r = gact_r[h:h + 1, :]
        m_prev = m_ref[h:h + 1, 0:1]

        dmat = jnp.where(causal, b_c + (ig_r - b_r), -1e30)
        m_inter = b_c + m_prev
        m_t = jnp.maximum(m_inter, jnp.max(dmat, axis=1, keepdims=True))
        wmat = jnp.exp(dmat - m_t)
        s_inter = jnp.exp(m_inter - m_t)
        scores = lax.dot_general(qb, kb, NT_DIMS, preferred_element_type=F32) * wmat
        ct = ct_ref[h]
        num = _dot(scores.astype(BF16), v_aug) + s_inter * _dot(qb, ct.astype(BF16))
        den = num[:, dh:dh + 1]
        hh = num[:, :dh] / jnp.maximum(jnp.abs(den), jnp.exp(-m_t))

        g_end = b_c[chunk - 1:chunk, :]
        a_c = g_end - b_c + ig_c
        m_new = jnp.maximum(g_end + m_prev, jnp.max(a_c, axis=0, keepdims=True))
        decay = jnp.exp(g_end + m_prev - m_new)
        wv = (jnp.exp(a_c - m_new) * v_aug.astype(F32)).astype(BF16)
        ct_ref[h] = decay * ct + lax.dot_general(kb, wv, TN_DIMS, preferred_element_type=F32)
        m_ref[h:h + 1, :] = jnp.broadcast_to(m_new, (1, LANES))

        ha = _sigmoid(o_ref[:, sl].astype(F32)) * hh
        ha = ha * lax.rsqrt(jnp.mean(ha * ha, axis=-1, keepdims=True) + EPS)
        y_ref[:, sl] = (ha * hng_ref[:, sl] * _silu(z_ref[:, sl].astype(F32))).astype(BF16)


def _mlstm(u, gc, gr, conv_w, brow, bcol, hn_g, tril, triu, batch, seq, chunk, heads, dh):
    n = batch * seq
    nc = seq // chunk
    width = heads * dh

    def col_block(c):
        return pl.BlockSpec((chunk, width), lambda b, i, c=c: (b * nc + i, c))

    def fixed(shape):
        return pl.BlockSpec(shape, lambda b, i: (0,) * len(shape))

    return pl.pallas_call(
        functools.partial(_mlstm_kernel, heads=heads, dh=dh),
        grid=(batch, nc),
        in_specs=[
            col_block(0), col_block(1), col_block(2), col_block(3), col_block(4),
            pl.BlockSpec((chunk, LANES), lambda b, i: (b * nc + i, 0)),
            pl.BlockSpec((GATE_ROWS, chunk), lambda b, i: (0, b * nc + i)),
            fixed((CONV_WIDTH, 2 * width)), fixed((1, LANES)), fixed((GATE_ROWS, 1)),
            fixed((1, width)), fixed((chunk, chunk)), fixed((chunk, chunk)),
        ],
        out_specs=pl.BlockSpec((chunk, width), lambda b, i: (b * nc + i, 0)),
        out_shape=jax.ShapeDtypeStruct((n, width), BF16),
        scratch_shapes=[
            pltpu.VMEM((heads, dh, dh + LANES), F32),
            pltpu.VMEM((8, LANES), F32),
            pltpu.VMEM((CONV_HALO, width), F32),
            pltpu.VMEM((CONV_HALO, width), F32),
        ],
        compiler_params=pltpu.CompilerParams(
            dimension_semantics=("arbitrary", "arbitrary"), vmem_limit_bytes=VMEM_LIMIT),
        name="even_mlstm",
    )(u, u, u, u, u, gc, gr, conv_w, brow, bcol, hn_g, tril, triu)


def _sb_kernel(q_ref, k_ref, v_ref, z_ref, m_ref, y_ref, out_ref, acc_ref):
    tq = q_ref.shape[0]
    i = pl.program_id(2)
    q = q_ref[...]
    mmat = m_ref[...]

    def block(kb, diagonal):
        start = pl.multiple_of(kb * tq, tq)
        kblk = k_ref[pl.ds(start, tq), :]
        vblk = v_ref[pl.ds(start, tq), :]
        z = lax.dot_general(q, kblk, NT_DIMS, preferred_element_type=F32)
        lneg = -_softplus(z)
        if diagonal:
            t_idx = lax.broadcasted_iota(jnp.int32, z.shape, 0)
            s_idx = lax.broadcasted_iota(jnp.int32, z.shape, 1)
            keep = s_idx < t_idx
            lneg = jnp.where(keep, lneg, 0.0)
        hi = lneg.astype(BF16)
        lo = (lneg - hi.astype(F32)).astype(BF16)
        after = _dot(hi, mmat) + _dot(lo, mmat)
        total = after[:, 0:1] + lneg[:, 0:1]
        if diagonal:
            a = jnp.where(keep, jnp.exp(z + lneg + after), 0.0)
            out_ref[...] = _dot(a.astype(BF16), vblk)
            acc_ref[...] = total
        else:
            acc = acc_ref[...]
            a = jnp.exp(z + lneg + (after + acc))
            out_ref[...] += _dot(a.astype(BF16), vblk)
            acc_ref[...] = acc + total

    block(i, True)

    def cond(carry):
        kb, top = carry
        return jnp.logical_and(kb >= 0, top > SB_SKIP_LOG)

    def body(carry):
        kb, _ = carry
        block(kb, False)
        return kb - 1, jnp.max(acc_ref[...])

    lax.while_loop(cond, body, (i - 1, jnp.max(acc_ref[...])))
    y_ref[...] = (out_ref[...] * _silu(z_ref[...].astype(F32))).astype(BF16)


def _stick_breaking(u, mmat, batch, seq, tq, heads, dh, q_col, k_col, v_col, z_col):
    n = batch * seq
    nq = seq // tq
    return pl.pallas_call(
        _sb_kernel,
        grid=(batch, heads, nq),
        in_specs=[
            pl.BlockSpec((tq, dh), lambda b, h, i: (b * nq + i, q_col + h)),
            pl.BlockSpec((seq, dh), lambda b, h, i: (b, k_col + h)),
            pl.BlockSpec((seq, dh), lambda b, h, i: (b, v_col + h)),
            pl.BlockSpec((tq, dh), lambda b, h, i: (b * nq + i, z_col + h)),
            pl.BlockSpec((tq, tq), lambda b, h, i: (0, 0)),
        ],
        out_specs=pl.BlockSpec((tq, dh), lambda b, h, i: (b * nq + i, h)),
        out_shape=jax.ShapeDtypeStruct((n, heads * dh), BF16),
        scratch_shapes=[pltpu.VMEM((tq, dh), F32), pltpu.VMEM((tq, 1), F32)],
        compiler_params=pltpu.CompilerParams(
            dimension_semantics=("arbitrary", "arbitrary", "arbitrary"), vmem_limit_bytes=VMEM_LIMIT),
        name="even_stick_breaking",
    )(u, u, u, u, mmat)


def _out_proj_kernel(ya_ref, yb_ref, wa_ref, wb_ref, x_ref, g_ref, o_ref):
    y = _dot(ya_ref[...], wa_ref[...]) + _dot(yb_ref[...], wb_ref[...])
    ms = jnp.mean(y * y, axis=-1, keepdims=True)
    o_ref[...] = x_ref[...] + y * lax.rsqrt(ms + EPS) * g_ref[...]


def _out_proj(ya, yb, wa, wb, x, g, tm):
    n, d = x.shape
    ka = ya.shape[1]
    kb = yb.shape[1]
    return pl.pallas_call(
        _out_proj_kernel,
        grid=(n // tm,),
        in_specs=[
            pl.BlockSpec((tm, ka), lambda i: (i, 0)),
            pl.BlockSpec((tm, kb), lambda i: (i, 0)),
            pl.BlockSpec((ka, d), lambda i: (0, 0)),
            pl.BlockSpec((kb, d), lambda i: (0, 0)),
            pl.BlockSpec((tm, d), lambda i: (i, 0)),
            pl.BlockSpec((1, d), lambda i: (0, 0)),
        ],
        out_specs=pl.BlockSpec((tm, d), lambda i: (i, 0)),
        out_shape=jax.ShapeDtypeStruct((n, d), F32),
        compiler_params=pltpu.CompilerParams(
            dimension_semantics=("arbitrary",), vmem_limit_bytes=VMEM_LIMIT),
        name="even_out_proj",
    )(ya, yb, wa, wb, x, g)


def _odd_kernel(x_ref, pre_ref, post_ref, wp_ref, wz_ref, pw_ref, ps_ref, wo_ref, o_ref, tail_ref,
                *, tiles_per_seq):
    tm = x_ref.shape[0]
    gdim = pw_ref.shape[1]
    tile = pl.program_id(0) % tiles_per_seq

    @pl.when(tile == 0)
    def _():
        tail_ref[...] = jnp.zeros_like(tail_ref)

    x = x_ref[...]
    ms = jnp.mean(x * x, axis=-1, keepdims=True)
    h = (x * lax.rsqrt(ms + EPS) * pre_ref[...]).astype(BF16)
    pos = tile * tm + lax.broadcasted_iota(jnp.int32, (tm, 1), 0)

    y = jnp.zeros((tm, o_ref.shape[1]), F32)
    for g, window in enumerate(POOL_WINDOWS):
        sl = slice(g * gdim, (g + 1) * gdim)
        p = _dot(h, wp_ref[:, sl])
        zg = _dot(h, wz_ref[:, sl])
        ext = jnp.concatenate([tail_ref[:, sl], p], axis=0)
        tail_ref[:, sl] = p[tm - POOL_HALO:, :]
        span = 1
        while span < window:
            ext = ext + pltpu.roll(ext, span, axis=0)
            span *= 2
        count = jnp.minimum(pos + 1, window).astype(F32)
        pooled = ext[POOL_HALO:, :] / count - p
        mixed = _dot(pooled.astype(BF16), pw_ref[g]) * ps_ref[:, sl]
        y = y + _dot((mixed * _silu(zg)).astype(BF16), wo_ref[sl, :])

    ms = jnp.mean(y * y, axis=-1, keepdims=True)
    o_ref[...] = x + y * lax.rsqrt(ms + EPS) * post_ref[...]


def _odd_layer(x, pre_g, post_g, wp, wz, pool_w, pool_scale, w_out, seq, tm):
    n, d = x.shape
    cw = wp.shape[1]
    groups, gdim, _ = pool_w.shape

    def fixed(shape):
        return pl.BlockSpec(shape, lambda i: (0,) * len(shape))

    return pl.pallas_call(
        functools.partial(_odd_kernel, tiles_per_seq=seq // tm),
        grid=(n // tm,),
        in_specs=[
            pl.BlockSpec((tm, d), lambda i: (i, 0)),
            fixed((1, d)), fixed((1, d)), fixed((d, cw)), fixed((d, cw)),
            fixed((groups, gdim, gdim)), fixed((1, cw)), fixed((cw, d)),
        ],
        out_specs=pl.BlockSpec((tm, d), lambda i: (i, 0)),
        out_shape=jax.ShapeDtypeStruct((n, d), F32),
        scratch_shapes=[pltpu.VMEM((POOL_HALO, cw), F32)],
        compiler_params=pltpu.CompilerParams(
            dimension_semantics=("arbitrary",), vmem_limit_bytes=VMEM_LIMIT),
        name="odd_layer",
    )(x, pre_g, post_g, wp, wz, pool_w, pool_scale, w_out)


def _pick(total, prefer):
    t = min(total, prefer)
    assert total % t == 0, (total, prefer)
    return t


def kernel(x, pre_norm_g, post_norm_g, w_in_ab, conv_qk, bias_i, bias_f, head_norm_g, w_out_ab,
           w_in_c, pool_w, pool_scale, w_out_c):
    batch, seq, d = x.shape
    depth = pre_norm_g.shape[0]
    a_width = head_norm_g.shape[1]
    a_dh = a_width // A_HEADS
    b_width = w_out_ab.shape[1] - a_width
    b_dh = b_width // B_HEADS
    c_width = pool_scale.shape[1]
    n = batch * seq

    chunk = _pick(seq, 256)
    tq = _pick(seq, 256)
    tm_in = _pick(n, 1024)
    tm_out = _pick(n, 512)
    tm_odd = _pick(seq, 512)

    idx = jnp.arange(chunk)
    tril = (idx[None, :] <= idx[:, None]).astype(BF16)
    triu = tril.T
    idq = jnp.arange(tq)
    later = (idq[:, None] > idq[None, :]).astype(BF16)

    gate0 = 5 * a_width
    b0 = gate0 + 2 * A_HEADS
    xf = x.reshape(n, d)
    for layer in range(depth):
        pre = pre_norm_g[layer][None, :]
        post = post_norm_g[layer][None, :]
        if layer % 2 == 0:
            e = layer // 2
            w = w_in_ab[e]
            w_main = jnp.concatenate(
                [w[:, :gate0], w[:, b0:b0 + b_width] * (b_dh ** -0.5), w[:, b0 + b_width:]],
                axis=1).astype(BF16)
            wg = w[:, gate0:b0]
            w_gate = jnp.pad(wg, ((0, 0), (0, LANES - 2 * A_HEADS))).astype(BF16)
            w_gate_t = jnp.pad(wg.T, ((0, GATE_ROWS - 2 * A_HEADS), (0, 0))).astype(BF16)
            u, gc, gr = _in_proj(xf, pre, w_main, w_gate, w_gate_t, tm_in, 1024)

            bias = jnp.concatenate([bias_i[e], bias_f[e]])
            brow = jnp.pad(bias, (0, LANES - 2 * A_HEADS))[None, :]
            bcol = jnp.pad(bias, (0, GATE_ROWS - 2 * A_HEADS))[:, None]
            ya = _mlstm(u, gc, gr, conv_qk[e], brow, bcol, head_norm_g[e][None, :], tril, triu,
                        batch, seq, chunk, A_HEADS, a_dh)
            sb0 = 5 * a_width // b_dh
            yb = _stick_breaking(u, later, batch, seq, tq, B_HEADS, b_dh,
                                 sb0, sb0 + B_HEADS, sb0 + 2 * B_HEADS, sb0 + 3 * B_HEADS)
            wo = w_out_ab[e].astype(BF16)
            xf = _out_proj(ya, yb, wo[:a_width], wo[a_width:], xf, post, tm_out)
        else:
            o = layer // 2
            wc = w_in_c[o].astype(BF16)
            xf = _odd_layer(xf, pre, post, wc[:, :c_width], wc[:, c_width:], pool_w[o].astype(BF16),
                            pool_scale[o][None, :], w_out_c[o].astype(BF16), seq, tm_odd)
    return xf.reshape(batch, seq, d)
```

```python
import functools

import jax
import jax.numpy as jnp
from jax import lax
from jax.experimental import pallas as pl
from jax.experimental.pallas import tpu as pltpu

F32 = jnp.float32
BF16 = jnp.bfloat16
EPS = 1e-6

A_HEADS = 4
B_HEADS = 8
CONV_WIDTH = 4
POOL_WINDOWS = (2, 4, 8, 16)
LANES = 128
GATE_ROWS = 16
POOL_HALO = 16
CONV_HALO = 8
SB_SUB = 128
SB_WIN = 256
SB_HEADS_PER_STEP = 2
SB_SKIP_LOG = -88.0
VMEM_LIMIT = 56 * 1024 * 1024

NT_DIMS = (((1,), (1,)), ((), ()))
TN_DIMS = (((0,), (0,)), ((), ()))


def _sigmoid(x):
    return 1.0 / (1.0 + jnp.exp(-x))


def _silu(x):
    return x * _sigmoid(x)


def _softplus(x):
    return jnp.maximum(x, 0.0) + jnp.log1p(jnp.exp(-jnp.abs(x)))


def _log_sigmoid(x):
    return -_softplus(-x)


def _split3(x):
    hi = x.astype(BF16)
    r1 = x - hi.astype(F32)
    mid = r1.astype(BF16)
    lo = (r1 - mid.astype(F32)).astype(BF16)
    return hi, mid, lo


def _dot(a, b):
    return jnp.dot(a, b, preferred_element_type=F32)


def _in_proj_kernel(x_ref, g_ref, w_ref, wg_ref, wgt_ref, u_ref, gc_ref, gr_ref, h_ref):
    @pl.when(pl.program_id(1) == 0)
    def _():
        x = x_ref[...]
        ms = jnp.mean(x * x, axis=-1, keepdims=True)
        h = (x * lax.rsqrt(ms + EPS) * g_ref[...]).astype(BF16)
        h_ref[...] = h
        gc_ref[...] = _dot(h, wg_ref[...])
        gr_ref[...] = lax.dot_general(wgt_ref[...], h, NT_DIMS,
                                      preferred_element_type=F32)

    u_ref[...] = _dot(h_ref[...], w_ref[...]).astype(BF16)


def _in_proj(x, g, w_main, w_gate, w_gate_t, tm, tn):
    n, d = x.shape
    cols = w_main.shape[1]
    return pl.pallas_call(
        _in_proj_kernel,
        grid=(n // tm, cols // tn),
        in_specs=[
            pl.BlockSpec((tm, d), lambda i, j: (i, 0)),
            pl.BlockSpec((1, d), lambda i, j: (0, 0)),
            pl.BlockSpec((d, tn), lambda i, j: (0, j)),
            pl.BlockSpec((d, LANES), lambda i, j: (0, 0)),
            pl.BlockSpec((GATE_ROWS, d), lambda i, j: (0, 0)),
        ],
        out_specs=[
            pl.BlockSpec((tm, tn), lambda i, j: (i, j)),
            pl.BlockSpec((tm, LANES), lambda i, j: (i, 0)),
            pl.BlockSpec((GATE_ROWS, tm), lambda i, j: (0, i)),
        ],
        out_shape=[
            jax.ShapeDtypeStruct((n, cols), BF16),
            jax.ShapeDtypeStruct((n, LANES), F32),
            jax.ShapeDtypeStruct((GATE_ROWS, n), F32),
        ],
        scratch_shapes=[pltpu.VMEM((tm, d), BF16)],
        compiler_params=pltpu.CompilerParams(
            dimension_semantics=("arbitrary", "arbitrary"), vmem_limit_bytes=VMEM_LIMIT),
        name="even_in_proj",
    )(x, g, w_main, w_gate, w_gate_t)


def _causal_conv(x, tail, w):
    rows = x.shape[0]
    ext = jnp.concatenate([tail, x], axis=0)
    y = x * w[CONV_WIDTH - 1:CONV_WIDTH, :]
    for k in range(1, CONV_WIDTH):
        shifted = pltpu.roll(ext, k, axis=0)[CONV_HALO:CONV_HALO + rows, :]
        y = y + shifted * w[CONV_WIDTH - 1 - k:CONV_WIDTH - k, :]
    return y


def _mlstm_kernel(q_ref, k_ref, v_ref, o_ref, z_ref, gc_ref, gr_ref, cw_ref, brow_ref, bcol_ref,
                  hng_ref, tril_ref, triu_ref, y_ref, ct_ref, m_ref, qt_ref, kt_ref, *, heads, dh):
    chunk = q_ref.shape[0]
    width = heads * dh

    @pl.when(pl.program_id(1) == 0)
    def _():
        ct_ref[...] = jnp.zeros_like(ct_ref)
        m_ref[...] = jnp.zeros_like(m_ref)
        qt_ref[...] = jnp.zeros_like(qt_ref)
        kt_ref[...] = jnp.zeros_like(kt_ref)

    gact_c = gc_ref[...] + brow_ref[...]
    gact_r = gr_ref[...] + bcol_ref[...]
    tril = tril_ref[...]
    triu = triu_ref[...]
    b_c_all = sum(_dot(tril, p) for p in _split3(_log_sigmoid(gact_c)))
    b_r_all = sum(_dot(p, triu) for p in _split3(_log_sigmoid(gact_r)))

    t_idx = lax.broadcasted_iota(jnp.int32, (chunk, chunk), 0)
    s_idx = lax.broadcasted_iota(jnp.int32, (chunk, chunk), 1)
    causal = s_idx <= t_idx
    ones_blk = jnp.where(lax.broadcasted_iota(jnp.int32, (chunk, LANES), 1) == 0, 1.0, 0.0).astype(BF16)

    for h in range(heads):
        sl = slice(h * dh, (h + 1) * dh)
        qpre = q_ref[:, sl].astype(F32)
        kpre = k_ref[:, sl].astype(F32)
        q = _silu(_causal_conv(qpre, qt_ref[:, sl], cw_ref[:, sl]))
        k = _silu(_causal_conv(kpre, kt_ref[:, sl], cw_ref[:, width + h * dh:width + (h + 1) * dh]))
        k = k * (dh ** -0.5)
        qt_ref[:, sl] = qpre[chunk - CONV_HALO:, :]
        kt_ref[:, sl] = kpre[chunk - CONV_HALO:, :]
        qb = q.astype(BF16)
        kb = k.astype(BF16)
        v_aug = jnp.concatenate([v_ref[:, sl], ones_blk], axis=1)

        b_c = b_c_all[:, heads + h:heads + h + 1]
        ig_c = gact_c[:, h:h + 1]
        b_r = b_r_all[heads + h:heads + h + 1, :]
        ig_r = gact_r[h:h + 1, :]
        m_prev = m_ref[h:h + 1, 0:1]

        dmat = jnp.where(causal, b_c + (ig_r - b_r), -1e30)
        m_inter = b_c + m_prev
        m_t = jnp.maximum(m_inter, jnp.max(dmat, axis=1, keepdims=True))
        wmat = jnp.exp(dmat - m_t)
        s_inter = jnp.exp(m_inter - m_t)
        scores = lax.dot_general(qb, kb, NT_DIMS, preferred_element_type=F32) * wmat
        ct = ct_ref[h]
        num = _dot(scores.astype(BF16), v_aug) + s_inter * _dot(qb, ct.astype(BF16))
        den = num[:, dh:dh + 1]
        hh = num[:, :dh] / jnp.maximum(jnp.abs(den), jnp.exp(-m_t))

        g_end = b_c[chunk - 1:chunk, :]
        a_c = g_end - b_c + ig_c
        m_new = jnp.maximum(g_end + m_prev, jnp.max(a_c, axis=0, keepdims=True))
        decay = jnp.exp(g_end + m_prev - m_new)
        kw = (k * jnp.exp(a_c - m_new)).astype(BF16)
        ct_ref[h] = decay * ct + lax.dot_general(kw, v_aug, TN_DIMS, preferred_element_type=F32)
        m_ref[h:h + 1, :] = jnp.broadcast_to(m_new, (1, LANES))

        ha = _sigmoid(o_ref[:, sl].astype(F32)) * hh
        ha = ha * lax.rsqrt(jnp.mean(ha * ha, axis=-1, keepdims=True) + EPS)
        y_ref[:, sl] = (ha * hng_ref[:, sl] * _silu(z_ref[:, sl].astype(F32))).astype(BF16)


def _mlstm(u, gc, gr, conv_w, brow, bcol, hn_g, tril, triu, batch, seq, chunk, heads, dh):
    n = batch * seq
    nc = seq // chunk
    width = heads * dh

    def col_block(c):
        return pl.BlockSpec((chunk, width), lambda b, i, c=c: (b * nc + i, c))

    def fixed(shape):
        return pl.BlockSpec(shape, lambda b, i: (0,) * len(shape))

    return pl.pallas_call(
        functools.partial(_mlstm_kernel, heads=heads, dh=dh),
        grid=(batch, nc),
        in_specs=[
            col_block(0), col_block(1), col_block(2), col_block(3), col_block(4),
            pl.BlockSpec((chunk, LANES), lambda b, i: (b * nc + i, 0)),
            pl.BlockSpec((GATE_ROWS, chunk), lambda b, i: (0, b * nc + i)),
            fixed((CONV_WIDTH, 2 * width)), fixed((1, LANES)), fixed((GATE_ROWS, 1)),
            fixed((1, width)), fixed((chunk, chunk)), fixed((chunk, chunk)),
        ],
        out_specs=pl.BlockSpec((chunk, width), lambda b, i: (b * nc + i, 0)),
        out_shape=jax.ShapeDtypeStruct((n, width), BF16),
        scratch_shapes=[
            pltpu.VMEM((heads, dh, dh + LANES), F32),
            pltpu.VMEM((8, LANES), F32),
            pltpu.VMEM((CONV_HALO, width), F32),
            pltpu.VMEM((CONV_HALO, width), F32),
        ],
        compiler_params=pltpu.CompilerParams(
            dimension_semantics=("arbitrary", "arbitrary"), vmem_limit_bytes=VMEM_LIMIT),
        name="even_mlstm",
    )(u, u, u, u, u, gc, gr, conv_w, brow, bcol, hn_g, tril, triu)


def _sb_windows(jobs, k_ref, v_ref, mmat):
    zs = [lax.dot_general(q, k_ref[pl.ds(start, SB_WIN), lanes], NT_DIMS, preferred_element_type=F32)
          for q, lanes, start, _, _ in jobs]
    lnegs = []
    for z, (_, _, _, keep, _) in zip(zs, jobs):
        softplus = jnp.maximum(z, 0.0) + jnp.log(1.0 + jnp.exp(-jnp.abs(z)))
        lnegs.append(jnp.where(keep, -softplus, 0.0))
    afters = []
    for lneg in lnegs:
        hi = lneg.astype(BF16)
        lo = (lneg - hi.astype(F32)).astype(BF16)
        afters.append(_dot(hi, mmat) + _dot(lo, mmat))
    results = []
    for z, lneg, after, (_, lanes, start, keep, acc) in zip(zs, lnegs, afters, jobs):
        total = after[:, 0:1] + lneg[:, 0:1]
        expo = z + lneg + after if acc is None else z + lneg + (after + acc)
        a = jnp.where(keep, jnp.exp(expo), 0.0)
        results.append((_dot(a.astype(BF16), v_ref[pl.ds(start, SB_WIN), lanes]), total))
    return results


def _sb_kernel(q_ref, k_ref, v_ref, z_ref, m_ref, y_ref, out_ref, acc_ref, *, dh):
    tq, width = q_ref.shape
    i = pl.program_id(2)
    mmat = m_ref[...]
    col = lax.broadcasted_iota(jnp.int32, (SB_SUB, SB_WIN), 1)
    col_minus_row = col - lax.broadcasted_iota(jnp.int32, (SB_SUB, SB_WIN), 0)
    tiles = [(sub, hh) for sub in range(tq // SB_SUB) for hh in range(width // dh)]

    def span(sub, hh):
        return slice(sub * SB_SUB, (sub + 1) * SB_SUB), slice(hh * dh, (hh + 1) * dh)

    jobs = []
    for sub, hh in tiles:
        rows, lanes = span(sub, hh)
        q0 = i * tq + sub * SB_SUB
        start = pl.multiple_of(jnp.maximum(q0 - (SB_WIN - SB_SUB), 0), SB_SUB)
        jobs.append((q_ref[rows, lanes], lanes, start, col_minus_row < (q0 - start), None))
    tops = []
    for n, ((sub, hh), (out, total)) in enumerate(zip(tiles, _sb_windows(jobs, k_ref, v_ref, mmat))):
        rows, lanes = span(sub, hh)
        out_ref[rows, lanes] = out
        acc_ref[n] = total
        tops.append(total)
    top = jnp.max(functools.reduce(jnp.maximum, tops))

    @pl.when(jnp.logical_and(i > 0, top > SB_SKIP_LOG))
    def _():
        for n, (sub, hh) in enumerate(tiles):
            rows, lanes = span(sub, hh)

            def cond(carry):
                limit, best = carry
                return jnp.logical_and(limit > 0, best > SB_SKIP_LOG)

            def body(carry, n=n, rows=rows, lanes=lanes):
                limit, _ = carry
                start = pl.multiple_of(jnp.maximum(limit - SB_WIN, 0), SB_SUB)
                keep = col < (limit - start)
                acc = acc_ref[n]
                (out, total), = _sb_windows([(q_ref[rows, lanes], lanes, start, keep, acc)],
                                            k_ref, v_ref, mmat)
                out_ref[rows, lanes] += out
                acc = acc + total
                acc_ref[n] = acc
                return start, jnp.max(acc)

            first = i * tq + sub * SB_SUB - (SB_WIN - SB_SUB)
            lax.while_loop(cond, body, (first, jnp.max(acc_ref[n])))

    y_ref[...] = (out_ref[...] * _silu(z_ref[...].astype(F32))).astype(BF16)


def _stick_breaking(u, mmat, batch, seq, tq, heads, dh, q_col, k_col, v_col, z_col):
    n = batch * seq
    nq = seq // tq
    width = SB_HEADS_PER_STEP * dh
    tiles = (tq // SB_SUB) * SB_HEADS_PER_STEP
    return pl.pallas_call(
        functools.partial(_sb_kernel, dh=dh),
        grid=(batch, heads // SB_HEADS_PER_STEP, nq),
        in_specs=[
            pl.BlockSpec((tq, width), lambda b, h, i: (b * nq + i, q_col + h)),
            pl.BlockSpec((seq, width), lambda b, h, i: (b, k_col + h)),
            pl.BlockSpec((seq, width), lambda b, h, i: (b, v_col + h)),
            pl.BlockSpec((tq, width), lambda b, h, i: (b * nq + i, z_col + h)),
            pl.BlockSpec((SB_WIN, SB_WIN), lambda b, h, i: (0, 0)),
        ],
        out_specs=pl.BlockSpec((tq, width), lambda b, h, i: (b * nq + i, h)),
        out_shape=jax.ShapeDtypeStruct((n, heads * dh), BF16),
        scratch_shapes=[pltpu.VMEM((tq, width), F32), pltpu.VMEM((tiles, SB_SUB, 1), F32)],
        compiler_params=pltpu.CompilerParams(
            dimension_semantics=("arbitrary", "arbitrary", "arbitrary"), vmem_limit_bytes=VMEM_LIMIT),
        name="even_stick_breaking",
    )(u, u, u, u, mmat)


def _out_proj_kernel(ya_ref, yb_ref, wa_ref, wb_ref, x_ref, g_ref, o_ref):
    y = _dot(ya_ref[...], wa_ref[...]) + _dot(yb_ref[...], wb_ref[...])
    ms = jnp.mean(y * y, axis=-1, keepdims=True)
    o_ref[...] = x_ref[...] + y * lax.rsqrt(ms + EPS) * g_ref[...]


def _out_proj(ya, yb, w, x, g, tm):
    n, d = x.shape
    ka = ya.shape[1]
    kb = yb.shape[1]
    assert ka == kb and w.shape == (ka + kb, d)
    return pl.pallas_call(
        _out_proj_kernel,
        grid=(n // tm,),
        in_specs=[
            pl.BlockSpec((tm, ka), lambda i: (i, 0)),
            pl.BlockSpec((tm, kb), lambda i: (i, 0)),
            pl.BlockSpec((ka, d), lambda i: (0, 0)),
            pl.BlockSpec((kb, d), lambda i: (1, 0)),
            pl.BlockSpec((tm, d), lambda i: (i, 0)),
            pl.BlockSpec((1, d), lambda i: (0, 0)),
        ],
        out_specs=pl.BlockSpec((tm, d), lambda i: (i, 0)),
        out_shape=jax.ShapeDtypeStruct((n, d), F32),
        compiler_params=pltpu.CompilerParams(
            dimension_semantics=("arbitrary",), vmem_limit_bytes=VMEM_LIMIT),
        name="even_out_proj",
    )(ya, yb, w, w, x, g)


def _odd_kernel(x_ref, pre_ref, post_ref, wp_ref, wz_ref, pw_ref, ps_ref, wo_ref, o_ref, tail_ref,
                *, tiles_per_seq):
    tm = x_ref.shape[0]
    gdim = pw_ref.shape[1]
    tile = pl.program_id(0) % tiles_per_seq

    @pl.when(tile == 0)
    def _():
        tail_ref[...] = jnp.zeros_like(tail_ref)

    x = x_ref[...]
    ms = jnp.mean(x * x, axis=-1, keepdims=True)
    h = (x * lax.rsqrt(ms + EPS) * pre_ref[...]).astype(BF16)
    pos = tile * tm + lax.broadcasted_iota(jnp.int32, (tm, 1), 0)

    y = jnp.zeros((tm, o_ref.shape[1]), F32)
    for g, window in enumerate(POOL_WINDOWS):
        sl = slice(g * gdim, (g + 1) * gdim)
        p = _dot(h, wp_ref[:, sl])
        zg = _dot(h, wz_ref[:, sl])
        ext = jnp.concatenate([tail_ref[:, sl], p], axis=0)
        tail_ref[:, sl] = p[tm - POOL_HALO:, :]
        span = 1
        while span < window:
            ext = ext + pltpu.roll(ext, span, axis=0)
            span *= 2
        count = jnp.minimum(pos + 1, window).astype(F32)
        pooled = ext[POOL_HALO:, :] / count - p
        mixed = _dot(pooled.astype(BF16), pw_ref[g]) * ps_ref[:, sl]
        y = y + _dot((mixed * _silu(zg)).astype(BF16), wo_ref[sl, :])

    ms = jnp.mean(y * y, axis=-1, keepdims=True)
    o_ref[...] = x + y * lax.rsqrt(ms + EPS) * post_ref[...]


def _odd_layer(x, pre_g, post_g, w_in, pool_w, pool_scale, w_out, seq, tm):
    n, d = x.shape
    cw = w_in.shape[1] // 2
    groups, gdim, _ = pool_w.shape

    def fixed(shape):
        return pl.BlockSpec(shape, lambda i: (0,) * len(shape))

    return pl.pallas_call(
        functools.partial(_odd_kernel, tiles_per_seq=seq // tm),
        grid=(n // tm,),
        in_specs=[
            pl.BlockSpec((tm, d), lambda i: (i, 0)),
            fixed((1, d)), fixed((1, d)), fixed((d, cw)), pl.BlockSpec((d, cw), lambda i: (0, 1)),
            fixed((groups, gdim, gdim)), fixed((1, cw)), fixed((cw, d)),
        ],
        out_specs=pl.BlockSpec((tm, d), lambda i: (i, 0)),
        out_shape=jax.ShapeDtypeStruct((n, d), F32),
        scratch_shapes=[pltpu.VMEM((POOL_HALO, cw), F32)],
        compiler_params=pltpu.CompilerParams(
            dimension_semantics=("arbitrary",), vmem_limit_bytes=VMEM_LIMIT),
        name="odd_layer",
    )(x, pre_g, post_g, w_in, w_in, pool_w, pool_scale, w_out)


def _pick(total, prefer):
    t = min(total, prefer)
    assert total % t == 0, (total, prefer)
    return t


def kernel(x, pre_norm_g, post_norm_g, w_in_ab, conv_qk, bias_i, bias_f, head_norm_g, w_out_ab,
           w_in_c, pool_w, pool_scale, w_out_c):
    batch, seq, d = x.shape
    depth = pre_norm_g.shape[0]
    a_width = head_norm_g.shape[1]
    a_dh = a_width // A_HEADS
    b_width = w_out_ab.shape[1] - a_width
    b_dh = b_width // B_HEADS
    n = batch * seq

    chunk = _pick(seq, 256)
    tq = _pick(seq, 512)
    tm_in = _pick(n, 1024)
    tm_out = _pick(n, 1024)
    tm_odd = _pick(seq, 512)

    idx = jnp.arange(chunk)
    tril = (idx[None, :] <= idx[:, None]).astype(BF16)
    triu = (idx[:, None] <= idx[None, :]).astype(BF16)
    idw = jnp.arange(SB_WIN)
    later = (idw[:, None] > idw[None, :]).astype(BF16)

    gate0 = 5 * a_width
    b0 = gate0 + 2 * A_HEADS
    sb_cols = SB_HEADS_PER_STEP * b_dh
    sb0 = gate0 // sb_cols
    sb_step = b_width // sb_cols
    xf = x.reshape(n, d)
    for layer in range(depth):
        pre = pre_norm_g[layer][None, :]
        post = post_norm_g[layer][None, :]
        if layer % 2 == 0:
            e = layer // 2
            w = w_in_ab[e]
            w_main = jnp.concatenate(
                [w[:, :gate0], w[:, b0:b0 + b_width] * (b_dh ** -0.5), w[:, b0 + b_width:]],
                axis=1).astype(BF16)
            wg = w[:, gate0:b0]
            w_gate = jnp.pad(wg, ((0, 0), (0, LANES - 2 * A_HEADS))).astype(BF16)
            w_gate_t = jnp.pad(wg.T, ((0, GATE_ROWS - 2 * A_HEADS), (0, 0))).astype(BF16)
            u, gc, gr = _in_proj(xf, pre, w_main, w_gate, w_gate_t, tm_in, 3 * a_width)

            bias = jnp.concatenate([bias_i[e], bias_f[e]])
            brow = jnp.pad(bias, (0, LANES - 2 * A_HEADS))[None, :]
            bcol = jnp.pad(bias, (0, GATE_ROWS - 2 * A_HEADS))[:, None]
            ya = _mlstm(u, gc, gr, conv_qk[e], brow, bcol, head_norm_g[e][None, :], tril, triu,
                        batch, seq, chunk, A_HEADS, a_dh)
            yb = _stick_breaking(u, later, batch, seq, tq, B_HEADS, b_dh,
                                 sb0, sb0 + sb_step, sb0 + 2 * sb_step, sb0 + 3 * sb_step)
            xf = _out_proj(ya, yb, w_out_ab[e].astype(BF16), xf, post, tm_out)
        else:
            o = layer // 2
            xf = _odd_layer(xf, pre, post, w_in_c[o].astype(BF16), pool_w[o].astype(BF16),
                            pool_scale[o][None, :], w_out_c[o].astype(BF16), seq, tm_odd)
    return xf.reshape(batch, seq, d)
```

```python
import functools

import jax
import jax.numpy as jnp
from jax import lax
from jax.experimental import pallas as pl
from jax.experimental.pallas import tpu as pltpu

F32 = jnp.float32
BF16 = jnp.bfloat16
EPS = 1e-6

A_HEADS = 4
B_HEADS = 8
CONV_WIDTH = 4
POOL_WINDOWS = (2, 4, 8, 16)
LANES = 128
GATE_ROWS = 16
POOL_HALO = 16
CONV_HALO = 8
SB_SUB = 128
SB_WIN = 256
SB_HEADS_PER_STEP = 2
SB_SKIP_LOG = -88.0
SB_DONE = -3.0e38
VMEM_LIMIT = 56 * 1024 * 1024

NT_DIMS = (((1,), (1,)), ((), ()))
TN_DIMS = (((0,), (0,)), ((), ()))


def _sigmoid(x):
    return 0.5 * jnp.tanh(0.5 * x) + 0.5


def _silu(x):
    half = 0.5 * x
    return half * jnp.tanh(half) + half


def _softplus(x):
    return jnp.maximum(x, 0.0) + jnp.log1p(jnp.exp(-jnp.abs(x)))


def _log_sigmoid(x):
    return -_softplus(-x)


def _split3(x):
    hi = x.astype(BF16)
    r1 = x - hi.astype(F32)
    mid = r1.astype(BF16)
    lo = (r1 - mid.astype(F32)).astype(BF16)
    return hi, mid, lo


def _dot(a, b):
    return jnp.dot(a, b, preferred_element_type=F32)


def _in_proj_kernel(x_ref, g_ref, w_ref, wg_ref, wgt_ref, u_ref, gc_ref, gr_ref, h_ref):
    @pl.when(pl.program_id(1) == 0)
    def _():
        x = x_ref[...]
        ms = jnp.mean(x * x, axis=-1, keepdims=True)
        h = (x * lax.rsqrt(ms + EPS) * g_ref[...]).astype(BF16)
        h_ref[...] = h
        gc_ref[...] = _dot(h, wg_ref[...])
        gr_ref[...] = lax.dot_general(wgt_ref[...], h, NT_DIMS,
                                      preferred_element_type=F32)

    u_ref[...] = _dot(h_ref[...], w_ref[...]).astype(BF16)


def _in_proj(x, g, w_main, w_gate, w_gate_t, tm, tn):
    n, d = x.shape
    cols = w_main.shape[1]
    return pl.pallas_call(
        _in_proj_kernel,
        grid=(n // tm, cols // tn),
        in_specs=[
            pl.BlockSpec((tm, d), lambda i, j: (i, 0)),
            pl.BlockSpec((1, d), lambda i, j: (0, 0)),
            pl.BlockSpec((d, tn), lambda i, j: (0, j)),
            pl.BlockSpec((d, LANES), lambda i, j: (0, 0)),
            pl.BlockSpec((GATE_ROWS, d), lambda i, j: (0, 0)),
        ],
        out_specs=[
            pl.BlockSpec((tm, tn), lambda i, j: (i, j)),
            pl.BlockSpec((tm, LANES), lambda i, j: (i, 0)),
            pl.BlockSpec((GATE_ROWS, tm), lambda i, j: (0, i)),
        ],
        out_shape=[
            jax.ShapeDtypeStruct((n, cols), BF16),
            jax.ShapeDtypeStruct((n, LANES), F32),
            jax.ShapeDtypeStruct((GATE_ROWS, n), F32),
        ],
        scratch_shapes=[pltpu.VMEM((tm, d), BF16)],
        compiler_params=pltpu.CompilerParams(
            dimension_semantics=("arbitrary", "arbitrary"), vmem_limit_bytes=VMEM_LIMIT),
        name="even_in_proj",
    )(x, g, w_main, w_gate, w_gate_t)


def _conv_silu(x_ref, cols, w_ref, w_col0, hist_ref):
    rows = x_ref.shape[0]
    outs = []
    for c in range(cols.start // LANES, cols.stop // LANES):
        hist_ref[c, CONV_HALO:, :] = x_ref[:, c * LANES:(c + 1) * LANES].astype(F32)
        wl = slice(w_col0 + c * LANES, w_col0 + (c + 1) * LANES)
        y = None
        for k in range(CONV_WIDTH):
            tap = w_ref[CONV_WIDTH - 1 - k:CONV_WIDTH - k, wl]
            term = hist_ref[c, CONV_HALO - k:CONV_HALO - k + rows, :] * tap
            y = term if y is None else y + term
        hist_ref[c, 0:CONV_HALO, :] = hist_ref[c, rows:rows + CONV_HALO, :]
        outs.append(_silu(y))
    return jnp.concatenate(outs, axis=1)


def _mlstm_kernel(q_ref, k_ref, v_ref, o_ref, z_ref, gc_ref, gr_ref, cw_ref, brow_ref, bcol_ref,
                  hng_ref, tril_ref, triu_ref, y_ref, ct_ref, m_ref, qh_ref, kh_ref, *, heads, dh):
    chunk = q_ref.shape[0]
    width = heads * dh

    @pl.when(pl.program_id(1) == 0)
    def _():
        ct_ref[...] = jnp.zeros_like(ct_ref)
        m_ref[...] = jnp.zeros_like(m_ref)
        qh_ref[...] = jnp.zeros_like(qh_ref)
        kh_ref[...] = jnp.zeros_like(kh_ref)

    gact_c = gc_ref[...] + brow_ref[...]
    gact_r = gr_ref[...] + bcol_ref[...]
    tril = tril_ref[...]
    triu = triu_ref[...]
    b_c_all = sum(_dot(tril, p) for p in _split3(_log_sigmoid(gact_c)))
    b_r_all = sum(_dot(p, triu) for p in _split3(_log_sigmoid(gact_r)))

    t_idx = lax.broadcasted_iota(jnp.int32, (chunk, chunk), 0)
    s_idx = lax.broadcasted_iota(jnp.int32, (chunk, chunk), 1)
    causal = s_idx <= t_idx
    ones_blk = jnp.where(lax.broadcasted_iota(jnp.int32, (chunk, LANES), 1) == 0, 1.0, 0.0).astype(BF16)

    for h in range(heads):
        sl = slice(h * dh, (h + 1) * dh)
        q = _conv_silu(q_ref, sl, cw_ref, 0, qh_ref)
        k = _conv_silu(k_ref, sl, cw_ref, width, kh_ref) * (dh ** -0.5)
        qb = q.astype(BF16)
        kb = k.astype(BF16)
        v_aug = jnp.concatenate([v_ref[:, sl], ones_blk], axis=1)

        b_c = b_c_all[:, heads + h:heads + h + 1]
        ig_c = gact_c[:, h:h + 1]
        b_r = b_r_all[heads + h:heads + h + 1, :]
        ig_r = gact_r[h:h + 1, :]
        m_prev = m_ref[h:h + 1, 0:1]

        dmat = jnp.where(causal, b_c + (ig_r - b_r), -1e30)
        m_inter = b_c + m_prev
        m_t = jnp.maximum(m_inter, jnp.max(dmat, axis=1, keepdims=True))
        wmat = jnp.exp(dmat - m_t)
        s_inter = jnp.exp(m_inter - m_t)
        scores = lax.dot_general(qb, kb, NT_DIMS, preferred_element_type=F32) * wmat
        ct = ct_ref[h]
        num = _dot(scores.astype(BF16), v_aug) + s_inter * _dot(qb, ct.astype(BF16))
        den = num[:, dh:dh + 1]
        hh = num[:, :dh] / jnp.maximum(jnp.abs(den), jnp.exp(-m_t))

        g_end = b_c[chunk - 1:chunk, :]
        a_c = g_end - b_c + ig_c
        m_new = jnp.maximum(g_end + m_prev, jnp.max(a_c, axis=0, keepdims=True))
        decay = jnp.exp(g_end + m_prev - m_new)
        kw = (k * jnp.exp(a_c - m_new)).astype(BF16)
        ct_ref[h] = decay * ct + lax.dot_general(kw, v_aug, TN_DIMS, preferred_element_type=F32)
        m_ref[h:h + 1, :] = jnp.broadcast_to(m_new, (1, LANES))

        ha = _sigmoid(o_ref[:, sl].astype(F32)) * hh
        ha = ha * lax.rsqrt(jnp.mean(ha * ha, axis=-1, keepdims=True) + EPS)
        y_ref[:, sl] = (ha * hng_ref[:, sl] * _silu(z_ref[:, sl].astype(F32))).astype(BF16)


def _mlstm(u, gc, gr, conv_w, brow, bcol, hn_g, tril, triu, batch, seq, chunk, heads, dh):
    n = batch * seq
    nc = seq // chunk
    width = heads * dh

    def col_block(c):
        return pl.BlockSpec((chunk, width), lambda b, i, c=c: (b * nc + i, c))

    def fixed(shape):
        return pl.BlockSpec(shape, lambda b, i: (0,) * len(shape))

    return pl.pallas_call(
        functools.partial(_mlstm_kernel, heads=heads, dh=dh),
        grid=(batch, nc),
        in_specs=[
            col_block(0), col_block(1), col_block(2), col_block(3), col_block(4),
            pl.BlockSpec((chunk, LANES), lambda b, i: (b * nc + i, 0)),
            pl.BlockSpec((GATE_ROWS, chunk), lambda b, i: (0, b * nc + i)),
            fixed((CONV_WIDTH, 2 * width)), fixed((1, LANES)), fixed((GATE_ROWS, 1)),
            fixed((1, width)), fixed((chunk, chunk)), fixed((chunk, chunk)),
        ],
        out_specs=pl.BlockSpec((chunk, width), lambda b, i: (b * nc + i, 0)),
        out_shape=jax.ShapeDtypeStruct((n, width), BF16),
        scratch_shapes=[
            pltpu.VMEM((heads, dh, dh + LANES), F32),
            pltpu.VMEM((8, LANES), F32),
            pltpu.VMEM((width // LANES, CONV_HALO + chunk, LANES), F32),
            pltpu.VMEM((width // LANES, CONV_HALO + chunk, LANES), F32),
        ],
        compiler_params=pltpu.CompilerParams(
            dimension_semantics=("arbitrary", "arbitrary"), vmem_limit_bytes=VMEM_LIMIT),
        name="even_mlstm",
    )(u, u, u, u, u, gc, gr, conv_w, brow, bcol, hn_g, tril, triu)


def _sb_windows(jobs, k_ref, v_ref, mmat):
    zs = [lax.dot_general(q, k_ref[pl.ds(start, SB_WIN), lanes], NT_DIMS, preferred_element_type=F32)
          for q, lanes, start, _, _ in jobs]
    lnegs = []
    for z, (_, _, _, keep, _) in zip(zs, jobs):
        softplus = jnp.maximum(z, 0.0) + jnp.log(1.0 + jnp.exp(-jnp.abs(z)))
        lnegs.append(jnp.where(keep, -softplus, 0.0))
    afters = []
    for lneg in lnegs:
        hi = lneg.astype(BF16)
        lo = (lneg - hi.astype(F32)).astype(BF16)
        afters.append(_dot(hi, mmat) + _dot(lo, mmat))
    results = []
    for z, lneg, after, (_, lanes, start, keep, acc) in zip(zs, lnegs, afters, jobs):
        total = after[:, 0:1] + lneg[:, 0:1]
        expo = z + lneg + after if acc is None else z + lneg + (after + acc)
        a = jnp.where(keep, jnp.exp(expo), 0.0)
        results.append((_dot(a.astype(BF16), v_ref[pl.ds(start, SB_WIN), lanes]), total))
    return results


def _sb_kernel(q_ref, k_ref, v_ref, z_ref, m_ref, y_ref, out_ref, acc_ref, *, dh):
    tq, width = q_ref.shape
    i = pl.program_id(2)
    mmat = m_ref[...]
    col = lax.broadcasted_iota(jnp.int32, (SB_SUB, SB_WIN), 1)
    col_minus_row = col - lax.broadcasted_iota(jnp.int32, (SB_SUB, SB_WIN), 0)
    tiles = [(sub, hh) for sub in range(tq // SB_SUB) for hh in range(width // dh)]

    def span(sub, hh):
        return slice(sub * SB_SUB, (sub + 1) * SB_SUB), slice(hh * dh, (hh + 1) * dh)

    jobs = []
    for sub, hh in tiles:
        rows, lanes = span(sub, hh)
        q0 = i * tq + sub * SB_SUB
        start = pl.multiple_of(jnp.maximum(q0 - (SB_WIN - SB_SUB), 0), SB_SUB)
        jobs.append((q_ref[rows, lanes], lanes, start, col_minus_row < (q0 - start), None))
    pending = []
    for n, ((sub, hh), (_, _, start, _, _), (out, total)) in enumerate(
            zip(tiles, jobs, _sb_windows(jobs, k_ref, v_ref, mmat))):
        rows, lanes = span(sub, hh)
        out_ref[rows, lanes] = out
        acc_ref[n] = total
        pending.append(jnp.where(start > 0, total, SB_DONE))
    top = jnp.max(functools.reduce(jnp.maximum, pending))

    def more(carry):
        _, best = carry
        return best > SB_SKIP_LOG

    def one_round(carry):
        rnd, _ = carry
        jobs = []
        for n, (sub, hh) in enumerate(tiles):
            rows, lanes = span(sub, hh)
            limit = i * tq + sub * SB_SUB - (SB_WIN - SB_SUB) - rnd * SB_WIN
            start = pl.multiple_of(jnp.maximum(limit - SB_WIN, 0), SB_SUB)
            jobs.append((q_ref[rows, lanes], lanes, start, col < (limit - start), acc_ref[n]))
        pending = []
        for n, ((sub, hh), (_, _, start, _, acc), (out, total)) in enumerate(
                zip(tiles, jobs, _sb_windows(jobs, k_ref, v_ref, mmat))):
            rows, lanes = span(sub, hh)
            out_ref[rows, lanes] += out
            acc = acc + total
            acc_ref[n] = acc
            pending.append(jnp.where(start > 0, acc, SB_DONE))
        return rnd + 1, jnp.max(functools.reduce(jnp.maximum, pending))

    lax.while_loop(more, one_round, (0, top))
    y_ref[...] = (out_ref[...] * _silu(z_ref[...].astype(F32))).astype(BF16)


def _stick_breaking(u, mmat, batch, seq, tq, heads, dh, q_col, k_col, v_col, z_col):
    n = batch * seq
    nq = seq // tq
    width = SB_HEADS_PER_STEP * dh
    tiles = (tq // SB_SUB) * SB_HEADS_PER_STEP
    return pl.pallas_call(
        functools.partial(_sb_kernel, dh=dh),
        grid=(batch, heads // SB_HEADS_PER_STEP, nq),
        in_specs=[
            pl.BlockSpec((tq, width), lambda b, h, i: (b * nq + i, q_col + h)),
            pl.BlockSpec((seq, width), lambda b, h, i: (b, k_col + h)),
            pl.BlockSpec((seq, width), lambda b, h, i: (b, v_col + h)),
            pl.BlockSpec((tq, width), lambda b, h, i: (b * nq + i, z_col + h)),
            pl.BlockSpec((SB_WIN, SB_WIN), lambda b, h, i: (0, 0)),
        ],
        out_specs=pl.BlockSpec((tq, width), lambda b, h, i: (b * nq + i, h)),
        out_shape=jax.ShapeDtypeStruct((n, heads * dh), BF16),
        scratch_shapes=[pltpu.VMEM((tq, width), F32), pltpu.VMEM((tiles, SB_SUB, 1), F32)],
        compiler_params=pltpu.CompilerParams(
            dimension_semantics=("arbitrary", "arbitrary", "arbitrary"), vmem_limit_bytes=VMEM_LIMIT),
        name="even_stick_breaking",
    )(u, u, u, u, mmat)


def _out_proj_kernel(ya_ref, yb_ref, wa_ref, wb_ref, x_ref, g_ref, o_ref):
    y = _dot(ya_ref[...], wa_ref[...]) + _dot(yb_ref[...], wb_ref[...])
    ms = jnp.mean(y * y, axis=-1, keepdims=True)
    o_ref[...] = x_ref[...] + y * lax.rsqrt(ms + EPS) * g_ref[...]


def _out_proj(ya, yb, w, x, g, tm):
    n, d = x.shape
    ka = ya.shape[1]
    kb = yb.shape[1]
    assert ka == kb and w.shape == (ka + kb, d)
    return pl.pallas_call(
        _out_proj_kernel,
        grid=(n // tm,),
        in_specs=[
            pl.BlockSpec((tm, ka), lambda i: (i, 0)),
            pl.BlockSpec((tm, kb), lambda i: (i, 0)),
            pl.BlockSpec((ka, d), lambda i: (0, 0)),
            pl.BlockSpec((kb, d), lambda i: (1, 0)),
            pl.BlockSpec((tm, d), lambda i: (i, 0)),
            pl.BlockSpec((1, d), lambda i: (0, 0)),
        ],
        out_specs=pl.BlockSpec((tm, d), lambda i: (i, 0)),
        out_shape=jax.ShapeDtypeStruct((n, d), F32),
        compiler_params=pltpu.CompilerParams(
            dimension_semantics=("arbitrary",), vmem_limit_bytes=VMEM_LIMIT),
        name="even_out_proj",
    )(ya, yb, w, w, x, g)


def _odd_kernel(x_ref, pre_ref, post_ref, wp_ref, wz_ref, pw_ref, ps_ref, wo_ref, o_ref, tail_ref,
                *, tiles_per_seq):
    tm = x_ref.shape[0]
    gdim = pw_ref.shape[1]
    tile = pl.program_id(0) % tiles_per_seq

    @pl.when(tile == 0)
    def _():
        tail_ref[...] = jnp.zeros_like(tail_ref)

    x = x_ref[...]
    ms = jnp.mean(x * x, axis=-1, keepdims=True)
    h = (x * lax.rsqrt(ms + EPS) * pre_ref[...]).astype(BF16)
    pos = tile * tm + lax.broadcasted_iota(jnp.int32, (tm, 1), 0)

    y = jnp.zeros((tm, o_ref.shape[1]), F32)
    for g, window in enumerate(POOL_WINDOWS):
        sl = slice(g * gdim, (g + 1) * gdim)
        p = _dot(h, wp_ref[:, sl])
        zg = _dot(h, wz_ref[:, sl])
        ext = jnp.concatenate([tail_ref[:, sl], p], axis=0)
        tail_ref[:, sl] = p[tm - POOL_HALO:, :]
        span = 1
        while span < window:
            ext = ext + pltpu.roll(ext, span, axis=0)
            span *= 2
        count = jnp.minimum(pos + 1, window).astype(F32)
        pooled = ext[POOL_HALO:, :] / count - p
        mixed = _dot(pooled.astype(BF16), pw_ref[g]) * ps_ref[:, sl]
        y = y + _dot((mixed * _silu(zg)).astype(BF16), wo_ref[sl, :])

    ms = jnp.mean(y * y, axis=-1, keepdims=True)
    o_ref[...] = x + y * lax.rsqrt(ms + EPS) * post_ref[...]


def _odd_layer(x, pre_g, post_g, w_in, pool_w, pool_scale, w_out, seq, tm):
    n, d = x.shape
    cw = w_in.shape[1] // 2
    groups, gdim, _ = pool_w.shape

    def fixed(shape):
        return pl.BlockSpec(shape, lambda i: (0,) * len(shape))

    return pl.pallas_call(
        functools.partial(_odd_kernel, tiles_per_seq=seq // tm),
        grid=(n // tm,),
        in_specs=[
            pl.BlockSpec((tm, d), lambda i: (i, 0)),
            fixed((1, d)), fixed((1, d)), fixed((d, cw)), pl.BlockSpec((d, cw), lambda i: (0, 1)),
            fixed((groups, gdim, gdim)), fixed((1, cw)), fixed((cw, d)),
        ],
        out_specs=pl.BlockSpec((tm, d), lambda i: (i, 0)),
        out_shape=jax.ShapeDtypeStruct((n, d), F32),
        scratch_shapes=[pltpu.VMEM((POOL_HALO, cw), F32)],
        compiler_params=pltpu.CompilerParams(
            dimension_semantics=("arbitrary",), vmem_limit_bytes=VMEM_LIMIT),
        name="odd_layer",
    )(x, pre_g, post_g, w_in, w_in, pool_w, pool_scale, w_out)


def _pick(total, prefer):
    t = min(total, prefer)
    assert total % t == 0, (total, prefer)
    return t


def kernel(x, pre_norm_g, post_norm_g, w_in_ab, conv_qk, bias_i, bias_f, head_norm_g, w_out_ab,
           w_in_c, pool_w, pool_scale, w_out_c):
    batch, seq, d = x.shape
    depth = pre_norm_g.shape[0]
    a_width = head_norm_g.shape[1]
    a_dh = a_width // A_HEADS
    b_width = w_out_ab.shape[1] - a_width
    b_dh = b_width // B_HEADS
    n = batch * seq

    chunk = _pick(seq, 256)
    tq = _pick(seq, 512)
    tm_in = _pick(n, 1024)
    tm_out = _pick(n, 1024)
    tm_odd = _pick(seq, 512)

    idx = jnp.arange(chunk)
    tril = (idx[None, :] <= idx[:, None]).astype(BF16)
    triu = (idx[:, None] <= idx[None, :]).astype(BF16)
    idw = jnp.arange(SB_WIN)
    later = (idw[:, None] > idw[None, :]).astype(BF16)

    gate0 = 5 * a_width
    b0 = gate0 + 2 * A_HEADS
    sb_cols = SB_HEADS_PER_STEP * b_dh
    sb0 = gate0 // sb_cols
    sb_step = b_width // sb_cols
    xf = x.reshape(n, d)
    for layer in range(depth):
        pre = pre_norm_g[layer][None, :]
        post = post_norm_g[layer][None, :]
        if layer % 2 == 0:
            e = layer // 2
            w = w_in_ab[e]
            w_main = jnp.concatenate(
                [w[:, :gate0], w[:, b0:b0 + b_width] * (b_dh ** -0.5), w[:, b0 + b_width:]],
                axis=1).astype(BF16)
            wg = w[:, gate0:b0]
            w_gate = jnp.pad(wg, ((0, 0), (0, LANES - 2 * A_HEADS))).astype(BF16)
            w_gate_t = jnp.pad(wg.T, ((0, GATE_ROWS - 2 * A_HEADS), (0, 0))).astype(BF16)
            u, gc, gr = _in_proj(xf, pre, w_main, w_gate, w_gate_t, tm_in, 3 * a_width)

            bias = jnp.concatenate([bias_i[e], bias_f[e]])
            brow = jnp.pad(bias, (0, LANES - 2 * A_HEADS))[None, :]
            bcol = jnp.pad(bias, (0, GATE_ROWS - 2 * A_HEADS))[:, None]
            ya = _mlstm(u, gc, gr, conv_qk[e], brow, bcol, head_norm_g[e][None, :], tril, triu,
                        batch, seq, chunk, A_HEADS, a_dh)
            yb = _stick_breaking(u, later, batch, seq, tq, B_HEADS, b_dh,
                                 sb0, sb0 + sb_step, sb0 + 2 * sb_step, sb0 + 3 * sb_step)
            xf = _out_proj(ya, yb, w_out_ab[e].astype(BF16), xf, post, tm_out)
        else:
            o = layer // 2
            xf = _odd_layer(xf, pre, post, w_in_c[o].astype(BF16), pool_w[o].astype(BF16),
                            pool_scale[o][None, :], w_out_c[o].astype(BF16), seq, tm_odd)
    return xf.reshape(batch, seq, d)
```

```python
import functools

import jax
import jax.numpy as jnp
from jax import lax
from jax.experimental import pallas as pl
from jax.experimental.pallas import tpu as pltpu

F32 = jnp.float32
BF16 = jnp.bfloat16
EPS = 1e-6

A_HEADS = 4
B_HEADS = 8
CONV_WIDTH = 4
POOL_WINDOWS = (2, 4, 8, 16)
LANES = 128
GATE_ROWS = 16
POOL_HALO = 16
CONV_HALO = 8
SB_SUB = 128
SB_WIN = 256
SB_HEADS_PER_STEP = 2
SB_SKIP = 88.0
SB_DONE = 3.0e38
LOG2E = 1.4426950408889634
VMEM_LIMIT = 56 * 1024 * 1024

NT_DIMS = (((1,), (1,)), ((), ()))
TN_DIMS = (((0,), (0,)), ((), ()))


def _sigmoid(x):
    return 0.5 * jnp.tanh(0.5 * x) + 0.5


def _silu(x):
    half = 0.5 * x
    return half * jnp.tanh(half) + half


def _softplus(x):
    return jnp.maximum(x, 0.0) + jnp.log1p(jnp.exp(-jnp.abs(x)))


def _log_sigmoid(x):
    return -_softplus(-x)


def _split3(x):
    hi = x.astype(BF16)
    r1 = x - hi.astype(F32)
    mid = r1.astype(BF16)
    lo = (r1 - mid.astype(F32)).astype(BF16)
    return hi, mid, lo


def _dot(a, b):
    return jnp.dot(a, b, preferred_element_type=F32)


def _dot_nt(a, b):
    return lax.dot_general(a, b, NT_DIMS, preferred_element_type=F32)


def _in_proj_kernel(x_ref, g_ref, w_ref, wg_ref, cw_ref, u_ref, gc_ref, gr_ref, h_ref, hist_ref,
                    *, tiles_per_seq, conv_cols, k_scale):
    tm = x_ref.shape[0]

    @pl.when(pl.program_id(1) == 0)
    def _():
        @pl.when(pl.program_id(0) % tiles_per_seq == 0)
        def _():
            hist_ref[:, 0:CONV_HALO, :] = jnp.zeros((hist_ref.shape[0], CONV_HALO, LANES), F32)

        x = x_ref[...]
        ms = jnp.mean(x * x, axis=-1, keepdims=True)
        h = (x * lax.rsqrt(ms + EPS) * g_ref[...]).astype(BF16)
        h_ref[...] = h
        wg = wg_ref[...]
        gc_ref[...] = _dot_nt(h, wg)
        gr_ref[...] = _dot_nt(wg[:GATE_ROWS], h)

        pair = 2 * LANES
        for c2 in range(conv_cols // pair):
            r = _dot_nt(h, w_ref[c2 * pair:(c2 + 1) * pair, :])
            for half in range(2):
                c = 2 * c2 + half
                lanes = slice(c * LANES, (c + 1) * LANES)
                hist_ref[c, CONV_HALO:, :] = r[:, half * LANES:(half + 1) * LANES]
                y = None
                for k in range(CONV_WIDTH):
                    term = (hist_ref[c, CONV_HALO - k:CONV_HALO - k + tm, :]
                            * cw_ref[CONV_WIDTH - 1 - k:CONV_WIDTH - k, lanes])
                    y = term if y is None else y + term
                hist_ref[c, 0:CONV_HALO, :] = hist_ref[c, tm:tm + CONV_HALO, :]
                y = _silu(y)
                if c * LANES >= conv_cols // 2:
                    y = y * k_scale
                u_ref[:, lanes] = y.astype(BF16)
        u_ref[:, conv_cols:] = _dot_nt(h, w_ref[conv_cols:, :]).astype(BF16)

    @pl.when(pl.program_id(1) > 0)
    def _():
        u_ref[...] = _dot_nt(h_ref[...], w_ref[...]).astype(BF16)


def _in_proj(x, g, w_main_t, w_gate_t, conv_w, seq, tm, tn, conv_cols, k_scale):
    n, d = x.shape
    cols = w_main_t.shape[0]
    assert conv_cols <= tn and conv_cols % (2 * LANES) == 0
    return pl.pallas_call(
        functools.partial(_in_proj_kernel, tiles_per_seq=seq // tm, conv_cols=conv_cols, k_scale=k_scale),
        grid=(n // tm, cols // tn),
        in_specs=[
            pl.BlockSpec((tm, d), lambda i, j: (i, 0)),
            pl.BlockSpec((1, d), lambda i, j: (0, 0)),
            pl.BlockSpec((tn, d), lambda i, j: (j, 0)),
            pl.BlockSpec((LANES, d), lambda i, j: (0, 0)),
            pl.BlockSpec((CONV_WIDTH, conv_cols), lambda i, j: (0, 0)),
        ],
        out_specs=[
            pl.BlockSpec((tm, tn), lambda i, j: (i, j)),
            pl.BlockSpec((tm, LANES), lambda i, j: (i, 0)),
            pl.BlockSpec((GATE_ROWS, tm), lambda i, j: (0, i)),
        ],
        out_shape=[
            jax.ShapeDtypeStruct((n, cols), BF16),
            jax.ShapeDtypeStruct((n, LANES), F32),
            jax.ShapeDtypeStruct((GATE_ROWS, n), F32),
        ],
        scratch_shapes=[pltpu.VMEM((tm, d), BF16),
                        pltpu.VMEM((conv_cols // LANES, CONV_HALO + tm, LANES), F32)],
        compiler_params=pltpu.CompilerParams(
            dimension_semantics=("arbitrary", "arbitrary"), vmem_limit_bytes=VMEM_LIMIT),
        name="even_in_proj",
    )(x, g, w_main_t, w_gate_t, conv_w)


def _mlstm_kernel(q_ref, k_ref, v_ref, o_ref, z_ref, gc_ref, gr_ref, brow_ref, bcol_ref,
                  hng_ref, tril_ref, triu_ref, y_ref, ct_ref, m_ref, *, heads, dh):
    chunk = q_ref.shape[0]

    @pl.when(pl.program_id(1) == 0)
    def _():
        ct_ref[...] = jnp.zeros_like(ct_ref)
        m_ref[...] = jnp.zeros_like(m_ref)

    gact_c = gc_ref[...] + brow_ref[...]
    gact_r = gr_ref[...] + bcol_ref[...]
    tril = tril_ref[...]
    triu = triu_ref[...]
    b_c_all = sum(_dot(tril, p) for p in _split3(_log_sigmoid(gact_c)))
    b_r_all = sum(_dot(p, triu) for p in _split3(_log_sigmoid(gact_r)))

    t_idx = lax.broadcasted_iota(jnp.int32, (chunk, chunk), 0)
    s_idx = lax.broadcasted_iota(jnp.int32, (chunk, chunk), 1)
    causal = s_idx <= t_idx
    ones_blk = jnp.where(lax.broadcasted_iota(jnp.int32, (chunk, LANES), 1) == 0, 1.0, 0.0).astype(BF16)

    for h in range(heads):
        sl = slice(h * dh, (h + 1) * dh)
        qb = q_ref[:, sl]
        kb = k_ref[:, sl]
        v_aug = jnp.concatenate([v_ref[:, sl], ones_blk], axis=1)

        b_c = b_c_all[:, heads + h:heads + h + 1]
        ig_c = gact_c[:, h:h + 1]
        b_r = b_r_all[heads + h:heads + h + 1, :]
        ig_r = gact_r[h:h + 1, :]
        m_prev = m_ref[h:h + 1, 0:1]

        dmat = jnp.where(causal, b_c + (ig_r - b_r), -1e30)
        m_inter = b_c + m_prev
        m_t = jnp.maximum(m_inter, jnp.max(dmat, axis=1, keepdims=True))
        wmat = jnp.exp(dmat - m_t)
        s_inter = jnp.exp(m_inter - m_t)
        scores = _dot_nt(qb, kb) * wmat
        ct = ct_ref[h]
        num = _dot(scores.astype(BF16), v_aug) + s_inter * _dot(qb, ct.astype(BF16))
        den = num[:, dh:dh + 1]
        hh = num[:, :dh] / jnp.maximum(jnp.abs(den), jnp.exp(-m_t))

        g_end = b_c[chunk - 1:chunk, :]
        a_c = g_end - b_c + ig_c
        m_new = jnp.maximum(g_end + m_prev, jnp.max(a_c, axis=0, keepdims=True))
        decay = jnp.exp(g_end + m_prev - m_new)
        kw = (kb.astype(F32) * jnp.exp(a_c - m_new)).astype(BF16)
        ct_ref[h] = decay * ct + lax.dot_general(kw, v_aug, TN_DIMS, preferred_element_type=F32)
        m_ref[h:h + 1, :] = jnp.broadcast_to(m_new, (1, LANES))

        ha = _sigmoid(o_ref[:, sl].astype(F32)) * hh
        ha = ha * lax.rsqrt(jnp.mean(ha * ha, axis=-1, keepdims=True) + EPS)
        y_ref[:, sl] = (ha * hng_ref[:, sl] * _silu(z_ref[:, sl].astype(F32))).astype(BF16)


def _mlstm(u, gc, gr, brow, bcol, hn_g, tril, triu, batch, seq, chunk, heads, dh):
    n = batch * seq
    nc = seq // chunk
    width = heads * dh

    def col_block(c):
        return pl.BlockSpec((chunk, width), lambda b, i, c=c: (b * nc + i, c))

    def fixed(shape):
        return pl.BlockSpec(shape, lambda b, i: (0,) * len(shape))

    return pl.pallas_call(
        functools.partial(_mlstm_kernel, heads=heads, dh=dh),
        grid=(batch, nc),
        in_specs=[
            col_block(0), col_block(1), col_block(2), col_block(3), col_block(4),
            pl.BlockSpec((chunk, LANES), lambda b, i: (b * nc + i, 0)),
            pl.BlockSpec((GATE_ROWS, chunk), lambda b, i: (0, b * nc + i)),
            fixed((1, LANES)), fixed((GATE_ROWS, 1)),
            fixed((1, width)), fixed((chunk, chunk)), fixed((chunk, chunk)),
        ],
        out_specs=pl.BlockSpec((chunk, width), lambda b, i: (b * nc + i, 0)),
        out_shape=jax.ShapeDtypeStruct((n, width), BF16),
        scratch_shapes=[
            pltpu.VMEM((heads, dh, dh + LANES), F32),
            pltpu.VMEM((8, LANES), F32),
        ],
        compiler_params=pltpu.CompilerParams(
            dimension_semantics=("arbitrary", "arbitrary"), vmem_limit_bytes=VMEM_LIMIT),
        name="even_mlstm",
    )(u, u, u, u, u, gc, gr, brow, bcol, hn_g, tril, triu)


def _sb_windows(jobs, k_ref, v_ref, later):
    zs = [_dot_nt(q, k_ref[pl.ds(start, SB_WIN), lanes]) for q, lanes, start, _, _ in jobs]
    sps = []
    for z, (_, _, _, keep, _) in zip(zs, jobs):
        softplus = jnp.maximum(z, 0.0) + jnp.log(1.0 + jnp.exp2(jnp.abs(z) * -LOG2E))
        sps.append(jnp.where(keep, softplus, 0.0))
    afters = [_dot(sp.astype(BF16), later) for sp in sps]
    results = []
    for z, sp, after, (_, lanes, start, keep, acc) in zip(zs, sps, afters, jobs):
        total = after[:, 0:1] + sp[:, 0:1]
        expo = (z - sp) - after if acc is None else (z - sp) - (after + acc)
        a = jnp.where(keep, jnp.exp(expo), 0.0)
        results.append((_dot(a.astype(BF16), v_ref[pl.ds(start, SB_WIN), lanes]), total))
    return results


def _sb_kernel(q_ref, k_ref, v_ref, z_ref, m_ref, y_ref, out_ref, acc_ref, *, dh):
    tq, width = q_ref.shape
    i = pl.program_id(2)
    later = m_ref[...]
    col = lax.broadcasted_iota(jnp.int32, (SB_SUB, SB_WIN), 1)
    col_minus_row = col - lax.broadcasted_iota(jnp.int32, (SB_SUB, SB_WIN), 0)
    tiles = [(sub, hh) for sub in range(tq // SB_SUB) for hh in range(width // dh)]

    def span(sub, hh):
        return slice(sub * SB_SUB, (sub + 1) * SB_SUB), slice(hh * dh, (hh + 1) * dh)

    jobs = []
    for sub, hh in tiles:
        rows, lanes = span(sub, hh)
        q0 = i * tq + sub * SB_SUB
        start = pl.multiple_of(jnp.maximum(q0 - (SB_WIN - SB_SUB), 0), SB_SUB)
        jobs.append((q_ref[rows, lanes], lanes, start, col_minus_row < (q0 - start), None))
    pending = []
    for n, ((sub, hh), (_, _, start, _, _), (out, total)) in enumerate(
            zip(tiles, jobs, _sb_windows(jobs, k_ref, v_ref, later))):
        rows, lanes = span(sub, hh)
        out_ref[rows, lanes] = out
        acc_ref[n] = total
        pending.append(jnp.where(start > 0, total, SB_DONE))
    low = jnp.min(functools.reduce(jnp.minimum, pending))

    def more(carry):
        _, low = carry
        return low < SB_SKIP

    def one_round(carry):
        rnd, _ = carry
        jobs = []
        for n, (sub, hh) in enumerate(tiles):
            rows, lanes = span(sub, hh)
            limit = i * tq + sub * SB_SUB - (SB_WIN - SB_SUB) - rnd * SB_WIN
            start = pl.multiple_of(jnp.maximum(limit - SB_WIN, 0), SB_SUB)
            jobs.append((q_ref[rows, lanes], lanes, start, col < (limit - start), acc_ref[n]))
        pending = []
        for n, ((sub, hh), (_, _, start, _, acc), (out, total)) in enumerate(
                zip(tiles, jobs, _sb_windows(jobs, k_ref, v_ref, later))):
            rows, lanes = span(sub, hh)
            out_ref[rows, lanes] += out
            acc = acc + total
            acc_ref[n] = acc
            pending.append(jnp.where(start > 0, acc, SB_DONE))
        return rnd + 1, jnp.min(functools.reduce(jnp.minimum, pending))

    lax.while_loop(more, one_round, (0, low))
    y_ref[...] = (out_ref[...] * _silu(z_ref[...].astype(F32))).astype(BF16)


def _stick_breaking(u, later, batch, seq, tq, heads, dh, q_col, k_col, v_col, z_col):
    n = batch * seq
    nq = seq // tq
    width = SB_HEADS_PER_STEP * dh
    tiles = (tq // SB_SUB) * SB_HEADS_PER_STEP
    return pl.pallas_call(
        functools.partial(_sb_kernel, dh=dh),
        grid=(batch, heads // SB_HEADS_PER_STEP, nq),
        in_specs=[
            pl.BlockSpec((tq, width), lambda b, h, i: (b * nq + i, q_col + h)),
            pl.BlockSpec((seq, width), lambda b, h, i: (b, k_col + h)),
            pl.BlockSpec((seq, width), lambda b, h, i: (b, v_col + h)),
            pl.BlockSpec((tq, width), lambda b, h, i: (b * nq + i, z_col + h)),
            pl.BlockSpec((SB_WIN, SB_WIN), lambda b, h, i: (0, 0)),
        ],
        out_specs=pl.BlockSpec((tq, width), lambda b, h, i: (b * nq + i, h)),
        out_shape=jax.ShapeDtypeStruct((n, heads * dh), BF16),
        scratch_shapes=[pltpu.VMEM((tq, width), F32), pltpu.VMEM((tiles, SB_SUB, 1), F32)],
        compiler_params=pltpu.CompilerParams(
            dimension_semantics=("arbitrary", "arbitrary", "arbitrary"), vmem_limit_bytes=VMEM_LIMIT),
        name="even_stick_breaking",
    )(u, u, u, u, later)


def _out_proj_kernel(ya_ref, yb_ref, wa_ref, wb_ref, x_ref, g_ref, o_ref):
    y = _dot(ya_ref[...], wa_ref[...]) + _dot(yb_ref[...], wb_ref[...])
    ms = jnp.mean(y * y, axis=-1, keepdims=True)
    o_ref[...] = x_ref[...] + y * lax.rsqrt(ms + EPS) * g_ref[...]


def _out_proj(ya, yb, w, x, g, tm):
    n, d = x.shape
    ka = ya.shape[1]
    kb = yb.shape[1]
    assert ka == kb and w.shape == (ka + kb, d)
    return pl.pallas_call(
        _out_proj_kernel,
        grid=(n // tm,),
        in_specs=[
            pl.BlockSpec((tm, ka), lambda i: (i, 0)),
            pl.BlockSpec((tm, kb), lambda i: (i, 0)),
            pl.BlockSpec((ka, d), lambda i: (0, 0)),
            pl.BlockSpec((kb, d), lambda i: (1, 0)),
            pl.BlockSpec((tm, d), lambda i: (i, 0)),
            pl.BlockSpec((1, d), lambda i: (0, 0)),
        ],
        out_specs=pl.BlockSpec((tm, d), lambda i: (i, 0)),
        out_shape=jax.ShapeDtypeStruct((n, d), F32),
        compiler_params=pltpu.CompilerParams(
            dimension_semantics=("arbitrary",), vmem_limit_bytes=VMEM_LIMIT),
        name="even_out_proj",
    )(ya, yb, w, w, x, g)


def _odd_kernel(x_ref, pre_ref, post_ref, wp_ref, wz_ref, pw_ref, ps_ref, wo_ref, o_ref, tail_ref,
                *, tiles_per_seq):
    tm = x_ref.shape[0]
    gdim = pw_ref.shape[1]
    tile = pl.program_id(0) % tiles_per_seq

    @pl.when(tile == 0)
    def _():
        tail_ref[...] = jnp.zeros_like(tail_ref)

    x = x_ref[...]
    ms = jnp.mean(x * x, axis=-1, keepdims=True)
    h = (x * lax.rsqrt(ms + EPS) * pre_ref[...]).astype(BF16)
    pos = tile * tm + lax.broadcasted_iota(jnp.int32, (tm, 1), 0)

    y = jnp.zeros((tm, o_ref.shape[1]), F32)
    for g, window in enumerate(POOL_WINDOWS):
        sl = slice(g * gdim, (g + 1) * gdim)
        p = _dot(h, wp_ref[:, sl])
        zg = _dot(h, wz_ref[:, sl])
        ext = jnp.concatenate([tail_ref[:, sl], p], axis=0)
        tail_ref[:, sl] = p[tm - POOL_HALO:, :]
        span = 1
        while span < window:
            ext = ext + pltpu.roll(ext, span, axis=0)
            span *= 2
        count = jnp.minimum(pos + 1, window).astype(F32)
        pooled = ext[POOL_HALO:, :] / count - p
        mixed = _dot(pooled.astype(BF16), pw_ref[g]) * ps_ref[:, sl]
        y = y + _dot((mixed * _silu(zg)).astype(BF16), wo_ref[sl, :])

    ms = jnp.mean(y * y, axis=-1, keepdims=True)
    o_ref[...] = x + y * lax.rsqrt(ms + EPS) * post_ref[...]


def _odd_layer(x, pre_g, post_g, w_in, pool_w, pool_scale, w_out, seq, tm):
    n, d = x.shape
    cw = w_in.shape[1] // 2
    groups, gdim, _ = pool_w.shape

    def fixed(shape):
        return pl.BlockSpec(shape, lambda i: (0,) * len(shape))

    return pl.pallas_call(
        functools.partial(_odd_kernel, tiles_per_seq=seq // tm),
        grid=(n // tm,),
        in_specs=[
            pl.BlockSpec((tm, d), lambda i: (i, 0)),
            fixed((1, d)), fixed((1, d)), fixed((d, cw)), pl.BlockSpec((d, cw), lambda i: (0, 1)),
            fixed((groups, gdim, gdim)), fixed((1, cw)), fixed((cw, d)),
        ],
        out_specs=pl.BlockSpec((tm, d), lambda i: (i, 0)),
        out_shape=jax.ShapeDtypeStruct((n, d), F32),
        scratch_shapes=[pltpu.VMEM((POOL_HALO, cw), F32)],
        compiler_params=pltpu.CompilerParams(
            dimension_semantics=("arbitrary",), vmem_limit_bytes=VMEM_LIMIT),
        name="odd_layer",
    )(x, pre_g, post_g, w_in, w_in, pool_w, pool_scale, w_out)


def _pick(total, prefer):
    t = min(total, prefer)
    assert total % t == 0, (total, prefer)
    return t


def kernel(x, pre_norm_g, post_norm_g, w_in_ab, conv_qk, bias_i, bias_f, head_norm_g, w_out_ab,
           w_in_c, pool_w, pool_scale, w_out_c):
    batch, seq, d = x.shape
    depth = pre_norm_g.shape[0]
    a_width = head_norm_g.shape[1]
    a_dh = a_width // A_HEADS
    b_width = w_out_ab.shape[1] - a_width
    b_dh = b_width // B_HEADS
    n = batch * seq

    chunk = _pick(seq, 256)
    tq = _pick(seq, 512)
    tm_in = _pick(seq, 1024)
    tm_out = _pick(n, 1024)
    tm_odd = _pick(seq, 512)

    idx = jnp.arange(chunk)
    tril = (idx[None, :] <= idx[:, None]).astype(BF16)
    triu = (idx[:, None] <= idx[None, :]).astype(BF16)
    idw = jnp.arange(SB_WIN)
    later = (idw[:, None] > idw[None, :]).astype(BF16)

    gate0 = 5 * a_width
    b0 = gate0 + 2 * A_HEADS
    sb_cols = SB_HEADS_PER_STEP * b_dh
    sb0 = gate0 // sb_cols
    sb_step = b_width // sb_cols
    xf = x.reshape(n, d)
    for layer in range(depth):
        pre = pre_norm_g[layer][None, :]
        post = post_norm_g[layer][None, :]
        if layer % 2 == 0:
            e = layer // 2
            wt = jnp.swapaxes(w_in_ab[e], 0, 1)
            w_main_t = jnp.concatenate(
                [wt[:gate0], wt[b0:b0 + b_width] * (b_dh ** -0.5), wt[b0 + b_width:]], axis=0).astype(BF16)
            w_gate_t = jnp.pad(wt[gate0:b0], ((0, LANES - 2 * A_HEADS), (0, 0))).astype(BF16)
            u, gc, gr = _in_proj(xf, pre, w_main_t, w_gate_t, conv_qk[e], seq, tm_in, 3 * a_width,
                                 2 * a_width, a_dh ** -0.5)

            bias = jnp.concatenate([bias_i[e], bias_f[e]])
            brow = jnp.pad(bias, (0, LANES - 2 * A_HEADS))[None, :]
            bcol = jnp.pad(bias, (0, GATE_ROWS - 2 * A_HEADS))[:, None]
            ya = _mlstm(u, gc, gr, brow, bcol, head_norm_g[e][None, :], tril, triu,
                        batch, seq, chunk, A_HEADS, a_dh)
            yb = _stick_breaking(u, later, batch, seq, tq, B_HEADS, b_dh,
                                 sb0, sb0 + sb_step, sb0 + 2 * sb_step, sb0 + 3 * sb_step)
            xf = _out_proj(ya, yb, w_out_ab[e].astype(BF16), xf, post, tm_out)
        else:
            o = layer // 2
            xf = _odd_layer(xf, pre, post, w_in_c[o].astype(BF16), pool_w[o].astype(BF16),
                            pool_scale[o][None, :], w_out_c[o].astype(BF16), seq, tm_odd)
    return xf.reshape(batch, seq, d)
```

```python
import functools

import jax
import jax.numpy as jnp
from jax import lax
from jax.experimental import pallas as pl
from jax.experimental.pallas import tpu as pltpu

F32 = jnp.float32
BF16 = jnp.bfloat16
EPS = 1e-6

A_HEADS = 4
B_HEADS = 8
CONV_WIDTH = 4
POOL_WINDOWS = (2, 4, 8, 16)
LANES = 128
GATE_ROWS = 16
POOL_HALO = 16
CONV_HALO = 8
SB_SUB = 128
SB_WIN = 256
SB_HEADS_PER_STEP = 2
SB_SKIP = 88.0
SB_DONE = 3.0e38
LOG2E = 1.4426950408889634
VMEM_LIMIT = 56 * 1024 * 1024

NT_DIMS = (((1,), (1,)), ((), ()))
TN_DIMS = (((0,), (0,)), ((), ()))


def _sigmoid(x):
    return 0.5 * jnp.tanh(0.5 * x) + 0.5


def _silu(x):
    half = 0.5 * x
    return half * jnp.tanh(half) + half


def _softplus(x):
    return jnp.maximum(x, 0.0) + jnp.log1p(jnp.exp(-jnp.abs(x)))


def _log_sigmoid(x):
    return -_softplus(-x)


def _split3(x):
    hi = x.astype(BF16)
    r1 = x - hi.astype(F32)
    mid = r1.astype(BF16)
    lo = (r1 - mid.astype(F32)).astype(BF16)
    return hi, mid, lo


def _dot(a, b):
    return jnp.dot(a, b, preferred_element_type=F32)


def _dot_nt(a, b):
    return lax.dot_general(a, b, NT_DIMS, preferred_element_type=F32)


def _in_proj_kernel(x_ref, g_ref, w_ref, wg_ref, cw_ref, u_ref, gc_ref, gr_ref, h_ref, hist_ref,
                    *, tiles_per_seq, conv_cols, k_scale):
    tm = x_ref.shape[0]

    @pl.when(pl.program_id(1) == 0)
    def _():
        @pl.when(pl.program_id(0) % tiles_per_seq == 0)
        def _():
            hist_ref[:, 0:CONV_HALO, :] = jnp.zeros((hist_ref.shape[0], CONV_HALO, LANES), F32)

        x = x_ref[...]
        ms = jnp.mean(x * x, axis=-1, keepdims=True)
        h = (x * lax.rsqrt(ms + EPS) * g_ref[...]).astype(BF16)
        h_ref[...] = h
        wg = wg_ref[...]
        gc_ref[...] = _dot_nt(h, wg)
        gr_ref[...] = _dot_nt(wg[:GATE_ROWS], h)

        pair = 2 * LANES

        def project(c2):
            return _dot_nt(h, w_ref[c2 * pair:(c2 + 1) * pair, :])

        def stash(c2):
            r = project(c2)
            for half in range(2):
                hist_ref[2 * c2 + half, CONV_HALO:, :] = r[:, half * LANES:(half + 1) * LANES]

        def conv(c2):
            for c in (2 * c2, 2 * c2 + 1):
                lanes = slice(c * LANES, (c + 1) * LANES)
                y = None
                for k in range(CONV_WIDTH):
                    term = (hist_ref[c, CONV_HALO - k:CONV_HALO - k + tm, :]
                            * cw_ref[CONV_WIDTH - 1 - k:CONV_WIDTH - k, lanes])
                    y = term if y is None else y + term
                hist_ref[c, 0:CONV_HALO, :] = hist_ref[c, tm:tm + CONV_HALO, :]
                y = _silu(y)
                if c * LANES >= conv_cols // 2:
                    y = y * k_scale
                u_ref[:, lanes] = y.astype(BF16)

        n_conv = conv_cols // pair
        stash(0)
        for c2 in range(1, n_conv):
            stash(c2)
            conv(c2 - 1)
        for c2 in range(n_conv, u_ref.shape[1] // pair):
            u_ref[:, c2 * pair:(c2 + 1) * pair] = project(c2).astype(BF16)
            if c2 == n_conv:
                conv(n_conv - 1)

    @pl.when(pl.program_id(1) > 0)
    def _():
        u_ref[...] = _dot_nt(h_ref[...], w_ref[...]).astype(BF16)


def _in_proj(x, g, w_main_t, w_gate_t, conv_w, seq, tm, tn, conv_cols, k_scale):
    n, d = x.shape
    cols = w_main_t.shape[0]
    assert conv_cols < tn and conv_cols % (2 * LANES) == 0 and tn % (2 * LANES) == 0
    return pl.pallas_call(
        functools.partial(_in_proj_kernel, tiles_per_seq=seq // tm, conv_cols=conv_cols, k_scale=k_scale),
        grid=(n // tm, cols // tn),
        in_specs=[
            pl.BlockSpec((tm, d), lambda i, j: (i, 0)),
            pl.BlockSpec((1, d), lambda i, j: (0, 0)),
            pl.BlockSpec((tn, d), lambda i, j: (j, 0)),
            pl.BlockSpec((LANES, d), lambda i, j: (0, 0)),
            pl.BlockSpec((CONV_WIDTH, conv_cols), lambda i, j: (0, 0)),
        ],
        out_specs=[
            pl.BlockSpec((tm, tn), lambda i, j: (i, j)),
            pl.BlockSpec((tm, LANES), lambda i, j: (i, 0)),
            pl.BlockSpec((GATE_ROWS, tm), lambda i, j: (0, i)),
        ],
        out_shape=[
            jax.ShapeDtypeStruct((n, cols), BF16),
            jax.ShapeDtypeStruct((n, LANES), F32),
            jax.ShapeDtypeStruct((GATE_ROWS, n), F32),
        ],
        scratch_shapes=[pltpu.VMEM((tm, d), BF16),
                        pltpu.VMEM((conv_cols // LANES, CONV_HALO + tm, LANES), F32)],
        compiler_params=pltpu.CompilerParams(
            dimension_semantics=("arbitrary", "arbitrary"), vmem_limit_bytes=VMEM_LIMIT),
        name="even_in_proj",
    )(x, g, w_main_t, w_gate_t, conv_w)


def _mlstm_block(rows, q_ref, k_ref, v_ref, o_ref, z_ref, gc_ref, gr_ref, brow_ref, bcol_ref, hng_ref,
                 tril, triu, y_ref, cts, m_prevs, *, heads, dh):
    chunk = rows.stop - rows.start
    gact_c = gc_ref[rows, :] + brow_ref[...]
    gact_r = gr_ref[:, rows] + bcol_ref[...]
    b_c_all = sum(_dot(tril, p) for p in _split3(_log_sigmoid(gact_c)))
    b_r_all = sum(_dot(p, triu) for p in _split3(_log_sigmoid(gact_r)))

    t_idx = lax.broadcasted_iota(jnp.int32, (chunk, chunk), 0)
    s_idx = lax.broadcasted_iota(jnp.int32, (chunk, chunk), 1)
    causal = s_idx <= t_idx
    ones_blk = jnp.where(lax.broadcasted_iota(jnp.int32, (chunk, LANES), 1) == 0, 1.0, 0.0).astype(BF16)

    hs = range(heads)
    sls = [slice(h * dh, (h + 1) * dh) for h in hs]
    qbs = [q_ref[rows, sl] for sl in sls]
    kbs = [k_ref[rows, sl] for sl in sls]
    v_augs = [jnp.concatenate([v_ref[rows, sl], ones_blk], axis=1) for sl in sls]
    b_cs = [b_c_all[:, heads + h:heads + h + 1] for h in hs]

    raw_scores = [_dot_nt(qb, kb) for qb, kb in zip(qbs, kbs)]
    inters = [_dot(qb, ct.astype(BF16)) for qb, ct in zip(qbs, cts)]

    scores, m_ts, s_inters = [], [], []
    for h in hs:
        b_r = b_r_all[heads + h:heads + h + 1, :]
        ig_r = gact_r[h:h + 1, :]
        dmat = jnp.where(causal, b_cs[h] + (ig_r - b_r), -1e30)
        m_inter = b_cs[h] + m_prevs[h]
        m_t = jnp.maximum(m_inter, jnp.max(dmat, axis=1, keepdims=True))
        scores.append((raw_scores[h] * jnp.exp(dmat - m_t)).astype(BF16))
        m_ts.append(m_t)
        s_inters.append(jnp.exp(m_inter - m_t))

    nums = [_dot(scores[h], v_augs[h]) + s_inters[h] * inters[h] for h in hs]

    new_cts, new_ms = [], []
    for h in hs:
        g_end = b_cs[h][chunk - 1:chunk, :]
        a_c = g_end - b_cs[h] + gact_c[:, h:h + 1]
        m_new = jnp.maximum(g_end + m_prevs[h], jnp.max(a_c, axis=0, keepdims=True))
        decay = jnp.exp(g_end + m_prevs[h] - m_new)
        kw = (kbs[h].astype(F32) * jnp.exp(a_c - m_new)).astype(BF16)
        new_cts.append(decay * cts[h] + lax.dot_general(kw, v_augs[h], TN_DIMS, preferred_element_type=F32))
        new_ms.append(m_new)

    for h in hs:
        sl = sls[h]
        den = nums[h][:, dh:dh + 1]
        hh = nums[h][:, :dh] / jnp.maximum(jnp.abs(den), jnp.exp(-m_ts[h]))
        ha = _sigmoid(o_ref[rows, sl].astype(F32)) * hh
        ha = ha * lax.rsqrt(jnp.mean(ha * ha, axis=-1, keepdims=True) + EPS)
        y_ref[rows, sl] = (ha * hng_ref[:, sl] * _silu(z_ref[rows, sl].astype(F32))).astype(BF16)
    return new_cts, new_ms


def _mlstm_kernel(q_ref, k_ref, v_ref, o_ref, z_ref, gc_ref, gr_ref, brow_ref, bcol_ref,
                  hng_ref, tril_ref, triu_ref, y_ref, ct_ref, m_ref, *, heads, dh):
    chunk = tril_ref.shape[0]

    @pl.when(pl.program_id(1) == 0)
    def _():
        ct_ref[...] = jnp.zeros_like(ct_ref)
        m_ref[...] = jnp.zeros_like(m_ref)

    tril = tril_ref[...]
    triu = triu_ref[...]
    cts = [ct_ref[h] for h in range(heads)]
    ms = [m_ref[h:h + 1, 0:1] for h in range(heads)]
    for sub in range(q_ref.shape[0] // chunk):
        rows = slice(sub * chunk, (sub + 1) * chunk)
        cts, ms = _mlstm_block(rows, q_ref, k_ref, v_ref, o_ref, z_ref, gc_ref, gr_ref, brow_ref, bcol_ref,
                               hng_ref, tril, triu, y_ref, cts, ms, heads=heads, dh=dh)
    for h in range(heads):
        ct_ref[h] = cts[h]
        m_ref[h:h + 1, :] = jnp.broadcast_to(ms[h], (1, LANES))


def _mlstm(u, gc, gr, brow, bcol, hn_g, tril, triu, batch, seq, chunk, heads, dh):
    n = batch * seq
    nc = seq // chunk
    width = heads * dh
    blk = tril.shape[0]

    def col_block(c):
        return pl.BlockSpec((chunk, width), lambda b, i, c=c: (b * nc + i, c))

    def fixed(shape):
        return pl.BlockSpec(shape, lambda b, i: (0,) * len(shape))

    return pl.pallas_call(
        functools.partial(_mlstm_kernel, heads=heads, dh=dh),
        grid=(batch, nc),
        in_specs=[
            col_block(0), col_block(1), col_block(2), col_block(3), col_block(4),
            pl.BlockSpec((chunk, LANES), lambda b, i: (b * nc + i, 0)),
            pl.BlockSpec((GATE_ROWS, chunk), lambda b, i: (0, b * nc + i)),
            fixed((1, LANES)), fixed((GATE_ROWS, 1)),
            fixed((1, width)), fixed((blk, blk)), fixed((blk, blk)),
        ],
        out_specs=pl.BlockSpec((chunk, width), lambda b, i: (b * nc + i, 0)),
        out_shape=jax.ShapeDtypeStruct((n, width), BF16),
        scratch_shapes=[
            pltpu.VMEM((heads, dh, dh + LANES), F32),
            pltpu.VMEM((8, LANES), F32),
        ],
        compiler_params=pltpu.CompilerParams(
            dimension_semantics=("arbitrary", "arbitrary"), vmem_limit_bytes=VMEM_LIMIT),
        name="even_mlstm",
    )(u, u, u, u, u, gc, gr, brow, bcol, hn_g, tril, triu)


def _sb_windows(jobs, k_ref, v_ref, later):
    zs = [_dot_nt(q, k_ref[pl.ds(start, SB_WIN), lanes]) for q, lanes, start, _, _ in jobs]
    sps = []
    for z, (_, _, _, keep, _) in zip(zs, jobs):
        softplus = jnp.maximum(z, 0.0) + jnp.log(1.0 + jnp.exp2(jnp.abs(z) * -LOG2E))
        sps.append(jnp.where(keep, softplus, 0.0))
    afters = [_dot(sp.astype(BF16), later) for sp in sps]
    results = []
    for z, sp, after, (_, lanes, start, keep, acc) in zip(zs, sps, afters, jobs):
        total = after[:, 0:1] + sp[:, 0:1]
        expo = (z - sp) - after if acc is None else (z - sp) - (after + acc)
        a = jnp.where(keep, jnp.exp(expo), 0.0)
        results.append((_dot(a.astype(BF16), v_ref[pl.ds(start, SB_WIN), lanes]), total))
    return results


def _sb_kernel(q_ref, k_ref, v_ref, z_ref, m_ref, y_ref, out_ref, acc_ref, *, dh):
    tq, width = q_ref.shape
    i = pl.program_id(2)
    later = m_ref[...]
    col = lax.broadcasted_iota(jnp.int32, (SB_SUB, SB_WIN), 1)
    col_minus_row = col - lax.broadcasted_iota(jnp.int32, (SB_SUB, SB_WIN), 0)
    tiles = [(sub, hh) for sub in range(tq // SB_SUB) for hh in range(width // dh)]

    def span(sub, hh):
        return slice(sub * SB_SUB, (sub + 1) * SB_SUB), slice(hh * dh, (hh + 1) * dh)

    jobs = []
    for sub, hh in tiles:
        rows, lanes = span(sub, hh)
        q0 = i * tq + sub * SB_SUB
        start = pl.multiple_of(jnp.maximum(q0 - (SB_WIN - SB_SUB), 0), SB_SUB)
        jobs.append((q_ref[rows, lanes], lanes, start, col_minus_row < (q0 - start), None))
    pending = []
    for n, ((sub, hh), (_, _, start, _, _), (out, total)) in enumerate(
            zip(tiles, jobs, _sb_windows(jobs, k_ref, v_ref, later))):
        rows, lanes = span(sub, hh)
        out_ref[rows, lanes] = out
        acc_ref[n] = total
        pending.append(jnp.where(start > 0, total, SB_DONE))
    low = jnp.min(functools.reduce(jnp.minimum, pending))

    def more(carry):
        _, low = carry
        return low < SB_SKIP

    def one_round(carry):
        rnd, _ = carry
        jobs = []
        for n, (sub, hh) in enumerate(tiles):
            rows, lanes = span(sub, hh)
            limit = i * tq + sub * SB_SUB - (SB_WIN - SB_SUB) - rnd * SB_WIN
            start = pl.multiple_of(jnp.maximum(limit - SB_WIN, 0), SB_SUB)
            jobs.append((q_ref[rows, lanes], lanes, start, col < (limit - start), acc_ref[n]))
        pending = []
        for n, ((sub, hh), (_, _, start, _, acc), (out, total)) in enumerate(
                zip(tiles, jobs, _sb_windows(jobs, k_ref, v_ref, later))):
            rows, lanes = span(sub, hh)
            out_ref[rows, lanes] += out
            acc = acc + total
            acc_ref[n] = acc
            pending.append(jnp.where(start > 0, acc, SB_DONE))
        return rnd + 1, jnp.min(functools.reduce(jnp.minimum, pending))

    lax.while_loop(more, one_round, (0, low))
    y_ref[...] = (out_ref[...] * _silu(z_ref[...].astype(F32))).astype(BF16)


def _stick_breaking(u, later, batch, seq, tq, heads, dh, q_col, k_col, v_col, z_col):
    n = batch * seq
    nq = seq // tq
    width = SB_HEADS_PER_STEP * dh
    tiles = (tq // SB_SUB) * SB_HEADS_PER_STEP
    return pl.pallas_call(
        functools.partial(_sb_kernel, dh=dh),
        grid=(batch, heads // SB_HEADS_PER_STEP, nq),
        in_specs=[
            pl.BlockSpec((tq, width), lambda b, h, i: (b * nq + i, q_col + h)),
            pl.BlockSpec((seq, width), lambda b, h, i: (b, k_col + h)),
            pl.BlockSpec((seq, width), lambda b, h, i: (b, v_col + h)),
            pl.BlockSpec((tq, width), lambda b, h, i: (b * nq + i, z_col + h)),
            pl.BlockSpec((SB_WIN, SB_WIN), lambda b, h, i: (0, 0)),
        ],
        out_specs=pl.BlockSpec((tq, width), lambda b, h, i: (b * nq + i, h)),
        out_shape=jax.ShapeDtypeStruct((n, heads * dh), BF16),
        scratch_shapes=[pltpu.VMEM((tq, width), F32), pltpu.VMEM((tiles, SB_SUB, 1), F32)],
        compiler_params=pltpu.CompilerParams(
            dimension_semantics=("arbitrary", "arbitrary", "arbitrary"), vmem_limit_bytes=VMEM_LIMIT),
        name="even_stick_breaking",
    )(u, u, u, u, later)


def _out_proj_kernel(ya_ref, yb_ref, wa_ref, wb_ref, x_ref, g_ref, o_ref):
    y = _dot(ya_ref[...], wa_ref[...]) + _dot(yb_ref[...], wb_ref[...])
    ms = jnp.mean(y * y, axis=-1, keepdims=True)
    o_ref[...] = x_ref[...] + y * lax.rsqrt(ms + EPS) * g_ref[...]


def _out_proj(ya, yb, w, x, g, tm):
    n, d = x.shape
    ka = ya.shape[1]
    kb = yb.shape[1]
    assert ka == kb and w.shape == (ka + kb, d)
    return pl.pallas_call(
        _out_proj_kernel,
        grid=(n // tm,),
        in_specs=[
            pl.BlockSpec((tm, ka), lambda i: (i, 0)),
            pl.BlockSpec((tm, kb), lambda i: (i, 0)),
            pl.BlockSpec((ka, d), lambda i: (0, 0)),
            pl.BlockSpec((kb, d), lambda i: (1, 0)),
            pl.BlockSpec((tm, d), lambda i: (i, 0)),
            pl.BlockSpec((1, d), lambda i: (0, 0)),
        ],
        out_specs=pl.BlockSpec((tm, d), lambda i: (i, 0)),
        out_shape=jax.ShapeDtypeStruct((n, d), F32),
        compiler_params=pltpu.CompilerParams(
            dimension_semantics=("arbitrary",), vmem_limit_bytes=VMEM_LIMIT),
        name="even_out_proj",
    )(ya, yb, w, w, x, g)


def _odd_kernel(x_ref, pre_ref, post_ref, wp_ref, wz_ref, pw_ref, ps_ref, wo_ref, o_ref, tail_ref,
                *, tiles_per_seq):
    tm = x_ref.shape[0]
    gdim = pw_ref.shape[1]
    tile = pl.program_id(0) % tiles_per_seq

    @pl.when(tile == 0)
    def _():
        tail_ref[...] = jnp.zeros_like(tail_ref)

    x = x_ref[...]
    ms = jnp.mean(x * x, axis=-1, keepdims=True)
    h = (x * lax.rsqrt(ms + EPS) * pre_ref[...]).astype(BF16)
    pos = tile * tm + lax.broadcasted_iota(jnp.int32, (tm, 1), 0)

    y = jnp.zeros((tm, o_ref.shape[1]), F32)
    for g, window in enumerate(POOL_WINDOWS):
        sl = slice(g * gdim, (g + 1) * gdim)
        p = _dot(h, wp_ref[:, sl])
        zg = _dot(h, wz_ref[:, sl])
        ext = jnp.concatenate([tail_ref[:, sl], p], axis=0)
        tail_ref[:, sl] = p[tm - POOL_HALO:, :]
        span = 1
        while span < window:
            ext = ext + pltpu.roll(ext, span, axis=0)
            span *= 2
        count = jnp.minimum(pos + 1, window).astype(F32)
        pooled = ext[POOL_HALO:, :] / count - p
        mixed = _dot(pooled.astype(BF16), pw_ref[g]) * ps_ref[:, sl]
        y = y + _dot((mixed * _silu(zg)).astype(BF16), wo_ref[sl, :])

    ms = jnp.mean(y * y, axis=-1, keepdims=True)
    o_ref[...] = x + y * lax.rsqrt(ms + EPS) * post_ref[...]


def _odd_layer(x, pre_g, post_g, w_in, pool_w, pool_scale, w_out, seq, tm):
    n, d = x.shape
    cw = w_in.shape[1] // 2
    groups, gdim, _ = pool_w.shape

    def fixed(shape):
        return pl.BlockSpec(shape, lambda i: (0,) * len(shape))

    return pl.pallas_call(
        functools.partial(_odd_kernel, tiles_per_seq=seq // tm),
        grid=(n // tm,),
        in_specs=[
            pl.BlockSpec((tm, d), lambda i: (i, 0)),
            fixed((1, d)), fixed((1, d)), fixed((d, cw)), pl.BlockSpec((d, cw), lambda i: (0, 1)),
            fixed((groups, gdim, gdim)), fixed((1, cw)), fixed((cw, d)),
        ],
        out_specs=pl.BlockSpec((tm, d), lambda i: (i, 0)),
        out_shape=jax.ShapeDtypeStruct((n, d), F32),
        scratch_shapes=[pltpu.VMEM((POOL_HALO, cw), F32)],
        compiler_params=pltpu.CompilerParams(
            dimension_semantics=("arbitrary",), vmem_limit_bytes=VMEM_LIMIT),
        name="odd_layer",
    )(x, pre_g, post_g, w_in, w_in, pool_w, pool_scale, w_out)


def _pick(total, prefer):
    t = min(total, prefer)
    assert total % t == 0, (total, prefer)
    return t


def kernel(x, pre_norm_g, post_norm_g, w_in_ab, conv_qk, bias_i, bias_f, head_norm_g, w_out_ab,
           w_in_c, pool_w, pool_scale, w_out_c):
    batch, seq, d = x.shape
    depth = pre_norm_g.shape[0]
    a_width = head_norm_g.shape[1]
    a_dh = a_width // A_HEADS
    b_width = w_out_ab.shape[1] - a_width
    b_dh = b_width // B_HEADS
    n = batch * seq

    chunk = _pick(seq, 256)
    tq = _pick(seq, 512)
    tm_in = _pick(seq, 1024)
    tm_out = _pick(n, 1024)
    tm_odd = _pick(seq, 512)

    idx = jnp.arange(chunk)
    tril = (idx[None, :] <= idx[:, None]).astype(BF16)
    triu = (idx[:, None] <= idx[None, :]).astype(BF16)
    idw = jnp.arange(SB_WIN)
    later = (idw[:, None] > idw[None, :]).astype(BF16)

    gate0 = 5 * a_width
    b0 = gate0 + 2 * A_HEADS
    sb_cols = SB_HEADS_PER_STEP * b_dh
    sb0 = gate0 // sb_cols
    sb_step = b_width // sb_cols

    row_scale = jnp.ones((w_in_ab.shape[2],), F32).at[b0:b0 + b_width].set(b_dh ** -0.5)
    wt_all = jnp.swapaxes(w_in_ab, 1, 2) * row_scale[None, :, None]
    w_main_all = jnp.concatenate([wt_all[:, :gate0], wt_all[:, b0:]], axis=1).astype(BF16)
    w_gate_all = jnp.pad(wt_all[:, gate0:b0], ((0, 0), (0, LANES - 2 * A_HEADS), (0, 0))).astype(BF16)
    w_out_all = w_out_ab.astype(BF16)
    w_in_c_all = w_in_c.astype(BF16)
    pool_w_all = pool_w.astype(BF16)
    w_out_c_all = w_out_c.astype(BF16)

    xf = x.reshape(n, d)
    for layer in range(depth):
        pre = pre_norm_g[layer][None, :]
        post = post_norm_g[layer][None, :]
        if layer % 2 == 0:
            e = layer // 2
            u, gc, gr = _in_proj(xf, pre, w_main_all[e], w_gate_all[e], conv_qk[e], seq, tm_in, 3 * a_width,
                                 2 * a_width, a_dh ** -0.5)

            bias = jnp.concatenate([bias_i[e], bias_f[e]])
            brow = jnp.pad(bias, (0, LANES - 2 * A_HEADS))[None, :]
            bcol = jnp.pad(bias, (0, GATE_ROWS - 2 * A_HEADS))[:, None]
            ya = _mlstm(u, gc, gr, brow, bcol, head_norm_g[e][None, :], tril, triu,
                        batch, seq, _pick(seq, 2 * chunk), A_HEADS, a_dh)
            yb = _stick_breaking(u, later, batch, seq, tq, B_HEADS, b_dh,
                                 sb0, sb0 + sb_step, sb0 + 2 * sb_step, sb0 + 3 * sb_step)
            xf = _out_proj(ya, yb, w_out_all[e], xf, post, tm_out)
        else:
            o = layer // 2
            xf = _odd_layer(xf, pre, post, w_in_c_all[o], pool_w_all[o],
                            pool_scale[o][None, :], w_out_c_all[o], seq, tm_odd)
    return xf.reshape(batch, seq, d)
```

```python
import functools

import jax
import jax.numpy as jnp
from jax import lax
from jax.experimental import pallas as pl
from jax.experimental.pallas import tpu as pltpu

F32 = jnp.float32
BF16 = jnp.bfloat16
EPS = 1e-6

A_HEADS = 4
B_HEADS = 8
CONV_WIDTH = 4
POOL_WINDOWS = (2, 4, 8, 16)
LANES = 128
GATE_ROWS = 16
POOL_HALO = 16
CONV_HALO = 8
SB_SUB = 128
SB_WIN = 256
SB_HEADS_PER_STEP = 2
SB_SKIP = 88.0
SB_DONE = 3.0e38
LOG2E = 1.4426950408889634
VMEM_LIMIT = 56 * 1024 * 1024

NT_DIMS = (((1,), (1,)), ((), ()))
TN_DIMS = (((0,), (0,)), ((), ()))


def _sigmoid(x):
    return 0.5 * jnp.tanh(0.5 * x) + 0.5


def _silu(x):
    half = 0.5 * x
    return half * jnp.tanh(half) + half


def _softplus(x):
    return jnp.maximum(x, 0.0) + jnp.log1p(jnp.exp(-jnp.abs(x)))


def _log_sigmoid(x):
    return -_softplus(-x)


def _split3(x):
    hi = x.astype(BF16)
    r1 = x - hi.astype(F32)
    mid = r1.astype(BF16)
    lo = (r1 - mid.astype(F32)).astype(BF16)
    return hi, mid, lo


def _dot(a, b):
    return jnp.dot(a, b, preferred_element_type=F32)


def _dot_nt(a, b):
    return lax.dot_general(a, b, NT_DIMS, preferred_element_type=F32)


def _in_proj_kernel(x_ref, g_ref, w_ref, wg_ref, cw_ref, u_ref, gc_ref, gr_ref, h_ref, hist_ref,
                    *, tiles_per_seq, conv_cols, k_scale):
    tm = x_ref.shape[0]

    @pl.when(pl.program_id(1) == 0)
    def _():
        @pl.when(pl.program_id(0) % tiles_per_seq == 0)
        def _():
            hist_ref[:, 0:CONV_HALO, :] = jnp.zeros((hist_ref.shape[0], CONV_HALO, LANES), F32)

        x = x_ref[...]
        ms = jnp.mean(x * x, axis=-1, keepdims=True)
        h = (x * lax.rsqrt(ms + EPS) * g_ref[...]).astype(BF16)
        h_ref[...] = h
        wg = wg_ref[...]
        gc_ref[...] = _dot_nt(h, wg)
        gr_ref[...] = _dot_nt(wg[:GATE_ROWS], h)

        pair = 2 * LANES

        def project(c2):
            return _dot_nt(h, w_ref[c2 * pair:(c2 + 1) * pair, :])

        def stash(c2):
            r = project(c2)
            for half in range(2):
                hist_ref[2 * c2 + half, CONV_HALO:, :] = r[:, half * LANES:(half + 1) * LANES]

        def conv(c2):
            for c in (2 * c2, 2 * c2 + 1):
                lanes = slice(c * LANES, (c + 1) * LANES)
                y = None
                for k in range(CONV_WIDTH):
                    term = (hist_ref[c, CONV_HALO - k:CONV_HALO - k + tm, :]
                            * cw_ref[CONV_WIDTH - 1 - k:CONV_WIDTH - k, lanes])
                    y = term if y is None else y + term
                hist_ref[c, 0:CONV_HALO, :] = hist_ref[c, tm:tm + CONV_HALO, :]
                y = _silu(y)
                if c * LANES >= conv_cols // 2:
                    y = y * k_scale
                u_ref[:, lanes] = y.astype(BF16)

        n_conv = conv_cols // pair
        stash(0)
        for c2 in range(1, n_conv):
            stash(c2)
            conv(c2 - 1)
        for c2 in range(n_conv, u_ref.shape[1] // pair):
            u_ref[:, c2 * pair:(c2 + 1) * pair] = project(c2).astype(BF16)
            if c2 == n_conv:
                conv(n_conv - 1)

    @pl.when(pl.program_id(1) > 0)
    def _():
        u_ref[...] = _dot_nt(h_ref[...], w_ref[...]).astype(BF16)


def _in_proj(x, g, w_main_t, w_gate_t, conv_w, seq, tm, tn, conv_cols, k_scale):
    n, d = x.shape
    cols = w_main_t.shape[0]
    assert conv_cols < tn and conv_cols % (2 * LANES) == 0 and tn % (2 * LANES) == 0
    return pl.pallas_call(
        functools.partial(_in_proj_kernel, tiles_per_seq=seq // tm, conv_cols=conv_cols, k_scale=k_scale),
        grid=(n // tm, cols // tn),
        in_specs=[
            pl.BlockSpec((tm, d), lambda i, j: (i, 0)),
            pl.BlockSpec((1, d), lambda i, j: (0, 0)),
            pl.BlockSpec((tn, d), lambda i, j: (j, 0)),
            pl.BlockSpec((LANES, d), lambda i, j: (0, 0)),
            pl.BlockSpec((CONV_WIDTH, conv_cols), lambda i, j: (0, 0)),
        ],
        out_specs=[
            pl.BlockSpec((tm, tn), lambda i, j: (i, j)),
            pl.BlockSpec((tm, LANES), lambda i, j: (i, 0)),
            pl.BlockSpec((GATE_ROWS, tm), lambda i, j: (0, i)),
        ],
        out_shape=[
            jax.ShapeDtypeStruct((n, cols), BF16),
            jax.ShapeDtypeStruct((n, LANES), F32),
            jax.ShapeDtypeStruct((GATE_ROWS, n), F32),
        ],
        scratch_shapes=[pltpu.VMEM((tm, d), BF16),
                        pltpu.VMEM((conv_cols // LANES, CONV_HALO + tm, LANES), F32)],
        compiler_params=pltpu.CompilerParams(
            dimension_semantics=("arbitrary", "arbitrary"), vmem_limit_bytes=VMEM_LIMIT),
        name="even_in_proj",
    )(x, g, w_main_t, w_gate_t, conv_w)


def _mlstm_block(rows, q_ref, k_ref, v_ref, o_ref, z_ref, gc_ref, gr_ref, brow_ref, bcol_ref, hng_ref,
                 tril, triu, y_ref, cts, m_prevs, *, heads, dh):
    chunk = rows.stop - rows.start
    gact_c = gc_ref[rows, :] + brow_ref[...]
    gact_r = gr_ref[:, rows] + bcol_ref[...]
    b_c_all = sum(_dot(tril, p) for p in _split3(_log_sigmoid(gact_c)))
    b_r_all = sum(_dot(p, triu) for p in _split3(_log_sigmoid(gact_r)))

    t_idx = lax.broadcasted_iota(jnp.int32, (chunk, chunk), 0)
    s_idx = lax.broadcasted_iota(jnp.int32, (chunk, chunk), 1)
    causal = s_idx <= t_idx
    ones_blk = jnp.where(lax.broadcasted_iota(jnp.int32, (chunk, LANES), 1) == 0, 1.0, 0.0).astype(BF16)

    hs = range(heads)
    sls = [slice(h * dh, (h + 1) * dh) for h in hs]
    qbs = [q_ref[rows, sl] for sl in sls]
    kbs = [k_ref[rows, sl] for sl in sls]
    v_augs = [jnp.concatenate([v_ref[rows, sl], ones_blk], axis=1) for sl in sls]
    b_cs = [b_c_all[:, heads + h:heads + h + 1] for h in hs]

    raw_scores = [_dot_nt(qb, kb) for qb, kb in zip(qbs, kbs)]
    inters = [_dot(qb, ct.astype(BF16)) for qb, ct in zip(qbs, cts)]

    scores, m_ts, s_inters = [], [], []
    for h in hs:
        b_r = b_r_all[heads + h:heads + h + 1, :]
        ig_r = gact_r[h:h + 1, :]
        dmat = jnp.where(causal, b_cs[h] + (ig_r - b_r), -1e30)
        m_inter = b_cs[h] + m_prevs[h]
        m_t = jnp.maximum(m_inter, jnp.max(dmat, axis=1, keepdims=True))
        scores.append((raw_scores[h] * jnp.exp(dmat - m_t)).astype(BF16))
        m_ts.append(m_t)
        s_inters.append(jnp.exp(m_inter - m_t))

    nums = [_dot(scores[h], v_augs[h]) + s_inters[h] * inters[h] for h in hs]

    new_cts, new_ms = [], []
    for h in hs:
        g_end = b_cs[h][chunk - 1:chunk, :]
        a_c = g_end - b_cs[h] + gact_c[:, h:h + 1]
        m_new = jnp.maximum(g_end + m_prevs[h], jnp.max(a_c, axis=0, keepdims=True))
        decay = jnp.exp(g_end + m_prevs[h] - m_new)
        kw = (kbs[h].astype(F32) * jnp.exp(a_c - m_new)).astype(BF16)
        new_cts.append(decay * cts[h] + lax.dot_general(kw, v_augs[h], TN_DIMS, preferred_element_type=F32))
        new_ms.append(m_new)

    for h in hs:
        sl = sls[h]
        den = nums[h][:, dh:dh + 1]
        hh = nums[h][:, :dh] / jnp.maximum(jnp.abs(den), jnp.exp(-m_ts[h]))
        ha = _sigmoid(o_ref[rows, sl].astype(F32)) * hh
        ha = ha * lax.rsqrt(jnp.mean(ha * ha, axis=-1, keepdims=True) + EPS)
        y_ref[rows, sl] = (ha * hng_ref[:, sl] * _silu(z_ref[rows, sl].astype(F32))).astype(BF16)
    return new_cts, new_ms


def _mlstm_kernel(q_ref, k_ref, v_ref, o_ref, z_ref, gc_ref, gr_ref, brow_ref, bcol_ref,
                  hng_ref, tril_ref, triu_ref, y_ref, ct_ref, m_ref, *, heads, dh):
    chunk = tril_ref.shape[0]

    @pl.when(pl.program_id(1) == 0)
    def _():
        ct_ref[...] = jnp.zeros_like(ct_ref)
        m_ref[...] = jnp.zeros_like(m_ref)

    tril = tril_ref[...]
    triu = triu_ref[...]
    cts = [ct_ref[h] for h in range(heads)]
    ms = [m_ref[h:h + 1, 0:1] for h in range(heads)]
    for sub in range(q_ref.shape[0] // chunk):
        rows = slice(sub * chunk, (sub + 1) * chunk)
        cts, ms = _mlstm_block(rows, q_ref, k_ref, v_ref, o_ref, z_ref, gc_ref, gr_ref, brow_ref, bcol_ref,
                               hng_ref, tril, triu, y_ref, cts, ms, heads=heads, dh=dh)
    for h in range(heads):
        ct_ref[h] = cts[h]
        m_ref[h:h + 1, :] = jnp.broadcast_to(ms[h], (1, LANES))


def _mlstm(u, gc, gr, brow, bcol, hn_g, tril, triu, batch, seq, chunk, heads, dh):
    n = batch * seq
    nc = seq // chunk
    width = heads * dh
    blk = tril.shape[0]

    def col_block(c):
        return pl.BlockSpec((chunk, width), lambda b, i, c=c: (b * nc + i, c))

    def fixed(shape):
        return pl.BlockSpec(shape, lambda b, i: (0,) * len(shape))

    return pl.pallas_call(
        functools.partial(_mlstm_kernel, heads=heads, dh=dh),
        grid=(batch, nc),
        in_specs=[
            col_block(0), col_block(1), col_block(2), col_block(3), col_block(4),
            pl.BlockSpec((chunk, LANES), lambda b, i: (b * nc + i, 0)),
            pl.BlockSpec((GATE_ROWS, chunk), lambda b, i: (0, b * nc + i)),
            fixed((1, LANES)), fixed((GATE_ROWS, 1)),
            fixed((1, width)), fixed((blk, blk)), fixed((blk, blk)),
        ],
        out_specs=pl.BlockSpec((chunk, width), lambda b, i: (b * nc + i, 0)),
        out_shape=jax.ShapeDtypeStruct((n, width), BF16),
        scratch_shapes=[
            pltpu.VMEM((heads, dh, dh + LANES), F32),
            pltpu.VMEM((8, LANES), F32),
        ],
        compiler_params=pltpu.CompilerParams(
            dimension_semantics=("arbitrary", "arbitrary"), vmem_limit_bytes=VMEM_LIMIT),
        name="even_mlstm",
    )(u, u, u, u, u, gc, gr, brow, bcol, hn_g, tril, triu)


def _sb_windows(jobs, k_ref, v_ref, later):
    zs = [_dot_nt(q, k_ref[pl.ds(start, SB_WIN), lanes]) for q, lanes, start, _, _ in jobs]
    sps = []
    for z, (_, _, _, keep, _) in zip(zs, jobs):
        softplus = jnp.maximum(z, 0.0) + jnp.log(1.0 + jnp.exp2(jnp.abs(z) * -LOG2E))
        sps.append(jnp.where(keep, softplus, 0.0))
    afters = [_dot(sp.astype(BF16), later) for sp in sps]
    results = []
    for z, sp, after, (_, lanes, start, keep, acc) in zip(zs, sps, afters, jobs):
        total = after[:, 0:1] + sp[:, 0:1]
        expo = (z - sp) - after if acc is None else (z - sp) - (after + acc)
        a = jnp.where(keep, jnp.exp(expo), 0.0)
        results.append((_dot(a.astype(BF16), v_ref[pl.ds(start, SB_WIN), lanes]), total))
    return results


def _sb_kernel(q_ref, k_ref, v_ref, z_ref, m_ref, y_ref, out_ref, acc_ref, *, dh):
    tq, width = q_ref.shape
    i = pl.program_id(2)
    later = m_ref[...]
    col = lax.broadcasted_iota(jnp.int32, (SB_SUB, SB_WIN), 1)
    col_minus_row = col - lax.broadcasted_iota(jnp.int32, (SB_SUB, SB_WIN), 0)
    tiles = [(sub, hh) for sub in range(tq // SB_SUB) for hh in range(width // dh)]

    def span(sub, hh):
        return slice(sub * SB_SUB, (sub + 1) * SB_SUB), slice(hh * dh, (hh + 1) * dh)

    jobs = []
    for sub, hh in tiles:
        rows, lanes = span(sub, hh)
        q0 = i * tq + sub * SB_SUB
        start = pl.multiple_of(jnp.maximum(q0 - (SB_WIN - SB_SUB), 0), SB_SUB)
        jobs.append((q_ref[rows, lanes], lanes, start, col_minus_row < (q0 - start), None))
    pending = []
    for n, ((sub, hh), (_, _, start, _, _), (out, total)) in enumerate(
            zip(tiles, jobs, _sb_windows(jobs, k_ref, v_ref, later))):
        rows, lanes = span(sub, hh)
        out_ref[rows, lanes] = out
        acc_ref[n] = total
        pending.append(jnp.where(start > 0, total, SB_DONE))
    low = jnp.min(functools.reduce(jnp.minimum, pending))

    def more(carry):
        _, low = carry
        return low < SB_SKIP

    def one_round(carry):
        rnd, _ = carry
        jobs = []
        for n, (sub, hh) in enumerate(tiles):
            rows, lanes = span(sub, hh)
            limit = i * tq + sub * SB_SUB - (SB_WIN - SB_SUB) - rnd * SB_WIN
            start = pl.multiple_of(jnp.maximum(limit - SB_WIN, 0), SB_SUB)
            jobs.append((q_ref[rows, lanes], lanes, start, col < (limit - start), acc_ref[n]))
        pending = []
        for n, ((sub, hh), (_, _, start, _, acc), (out, total)) in enumerate(
                zip(tiles, jobs, _sb_windows(jobs, k_ref, v_ref, later))):
            rows, lanes = span(sub, hh)
            out_ref[rows, lanes] += out
            acc = acc + total
            acc_ref[n] = acc
            pending.append(jnp.where(start > 0, acc, SB_DONE))
        return rnd + 1, jnp.min(functools.reduce(jnp.minimum, pending))

    lax.while_loop(more, one_round, (0, low))
    y_ref[...] = (out_ref[...] * _silu(z_ref[...].astype(F32))).astype(BF16)


def _stick_breaking(u, later, batch, seq, tq, heads, dh, q_col, k_col, v_col, z_col):
    n = batch * seq
    nq = seq // tq
    width = SB_HEADS_PER_STEP * dh
    tiles = (tq // SB_SUB) * SB_HEADS_PER_STEP
    return pl.pallas_call(
        functools.partial(_sb_kernel, dh=dh),
        grid=(batch, heads // SB_HEADS_PER_STEP, nq),
        in_specs=[
            pl.BlockSpec((tq, width), lambda b, h, i: (b * nq + i, q_col + h)),
            pl.BlockSpec((seq, width), lambda b, h, i: (b, k_col + h)),
            pl.BlockSpec((seq, width), lambda b, h, i: (b, v_col + h)),
            pl.BlockSpec((tq, width), lambda b, h, i: (b * nq + i, z_col + h)),
            pl.BlockSpec((SB_WIN, SB_WIN), lambda b, h, i: (0, 0)),
        ],
        out_specs=pl.BlockSpec((tq, width), lambda b, h, i: (b * nq + i, h)),
        out_shape=jax.ShapeDtypeStruct((n, heads * dh), BF16),
        scratch_shapes=[pltpu.VMEM((tq, width), F32), pltpu.VMEM((tiles, SB_SUB, 1), F32)],
        compiler_params=pltpu.CompilerParams(
            dimension_semantics=("arbitrary", "arbitrary", "arbitrary"), vmem_limit_bytes=VMEM_LIMIT),
        name="even_stick_breaking",
    )(u, u, u, u, later)


def _out_proj_kernel(ya_ref, yb_ref, wa_ref, wb_ref, x_ref, g_ref, o_ref):
    y = _dot(ya_ref[...], wa_ref[...]) + _dot(yb_ref[...], wb_ref[...])
    ms = jnp.mean(y * y, axis=-1, keepdims=True)
    o_ref[...] = x_ref[...] + y * lax.rsqrt(ms + EPS) * g_ref[...]


def _out_proj(ya, yb, w, x, g, tm):
    n, d = x.shape
    ka = ya.shape[1]
    kb = yb.shape[1]
    assert ka == kb and w.shape == (ka + kb, d)
    return pl.pallas_call(
        _out_proj_kernel,
        grid=(n // tm,),
        in_specs=[
            pl.BlockSpec((tm, ka), lambda i: (i, 0)),
            pl.BlockSpec((tm, kb), lambda i: (i, 0)),
            pl.BlockSpec((ka, d), lambda i: (0, 0)),
            pl.BlockSpec((kb, d), lambda i: (1, 0)),
            pl.BlockSpec((tm, d), lambda i: (i, 0)),
            pl.BlockSpec((1, d), lambda i: (0, 0)),
        ],
        out_specs=pl.BlockSpec((tm, d), lambda i: (i, 0)),
        out_shape=jax.ShapeDtypeStruct((n, d), F32),
        compiler_params=pltpu.CompilerParams(
            dimension_semantics=("arbitrary",), vmem_limit_bytes=VMEM_LIMIT),
        name="even_out_proj",
    )(ya, yb, w, w, x, g)


def _odd_kernel(x_ref, pre_ref, post_ref, wp_ref, wz_ref, pw_ref, ps_ref, wo_ref, o_ref, tail_ref,
                *, tiles_per_seq):
    tm = x_ref.shape[0]
    gdim = pw_ref.shape[1]
    tile = pl.program_id(0) % tiles_per_seq

    @pl.when(tile == 0)
    def _():
        tail_ref[...] = jnp.zeros_like(tail_ref)

    x = x_ref[...]
    ms = jnp.mean(x * x, axis=-1, keepdims=True)
    h = (x * lax.rsqrt(ms + EPS) * pre_ref[...]).astype(BF16)
    pos = tile * tm + lax.broadcasted_iota(jnp.int32, (tm, 1), 0)

    y = jnp.zeros((tm, o_ref.shape[1]), F32)
    for g, window in enumerate(POOL_WINDOWS):
        sl = slice(g * gdim, (g + 1) * gdim)
        p = _dot(h, wp_ref[:, sl])
        zg = _dot(h, wz_ref[:, sl])
        ext = jnp.concatenate([tail_ref[:, sl], p], axis=0)
        tail_ref[:, sl] = p[tm - POOL_HALO:, :]
        span = 1
        while span < window:
            ext = ext + pltpu.roll(ext, span, axis=0)
            span *= 2
        count = jnp.minimum(pos + 1, window).astype(F32)
        pooled = ext[POOL_HALO:, :] / count - p
        mixed = _dot(pooled.astype(BF16), pw_ref[g]) * ps_ref[:, sl]
        y = y + _dot((mixed * _silu(zg)).astype(BF16), wo_ref[sl, :])

    ms = jnp.mean(y * y, axis=-1, keepdims=True)
    o_ref[...] = x + y * lax.rsqrt(ms + EPS) * post_ref[...]


def _odd_layer(x, pre_g, post_g, w_in, pool_w, pool_scale, w_out, seq, tm):
    n, d = x.shape
    cw = w_in.shape[1] // 2
    groups, gdim, _ = pool_w.shape

    def fixed(shape, index=None):
        index = index or (0,) * len(shape)
        return pl.BlockSpec(shape, lambda i: index, pipeline_mode=pl.Buffered(1))

    return pl.pallas_call(
        functools.partial(_odd_kernel, tiles_per_seq=seq // tm),
        grid=(n // tm,),
        in_specs=[
            pl.BlockSpec((tm, d), lambda i: (i, 0)),
            fixed((1, d)), fixed((1, d)), fixed((d, cw)), fixed((d, cw), (0, 1)),
            fixed((groups, gdim, gdim)), fixed((1, cw)), fixed((cw, d)),
        ],
        out_specs=pl.BlockSpec((tm, d), lambda i: (i, 0)),
        out_shape=jax.ShapeDtypeStruct((n, d), F32),
        scratch_shapes=[pltpu.VMEM((POOL_HALO, cw), F32)],
        compiler_params=pltpu.CompilerParams(
            dimension_semantics=("arbitrary",), vmem_limit_bytes=VMEM_LIMIT),
        name="odd_layer",
    )(x, pre_g, post_g, w_in, w_in, pool_w, pool_scale, w_out)


def _pick(total, prefer):
    t = min(total, prefer)
    assert total % t == 0, (total, prefer)
    return t


def kernel(x, pre_norm_g, post_norm_g, w_in_ab, conv_qk, bias_i, bias_f, head_norm_g, w_out_ab,
           w_in_c, pool_w, pool_scale, w_out_c):
    batch, seq, d = x.shape
    depth = pre_norm_g.shape[0]
    a_width = head_norm_g.shape[1]
    a_dh = a_width // A_HEADS
    b_width = w_out_ab.shape[1] - a_width
    b_dh = b_width // B_HEADS
    n = batch * seq

    chunk = _pick(seq, 256)
    tq = _pick(seq, 1024)
    tm_in = _pick(seq, 1024)
    tm_out = _pick(n, 1024)
    tm_odd = _pick(seq, 1024)

    idx = jnp.arange(chunk)
    tril = (idx[None, :] <= idx[:, None]).astype(BF16)
    triu = (idx[:, None] <= idx[None, :]).astype(BF16)
    idw = jnp.arange(SB_WIN)
    later = (idw[:, None] > idw[None, :]).astype(BF16)

    gate0 = 5 * a_width
    b0 = gate0 + 2 * A_HEADS
    sb_cols = SB_HEADS_PER_STEP * b_dh
    sb0 = gate0 // sb_cols
    sb_step = b_width // sb_cols

    row_scale = jnp.ones((w_in_ab.shape[2],), F32).at[b0:b0 + b_width].set(b_dh ** -0.5)
    wt_all = jnp.swapaxes(w_in_ab, 1, 2) * row_scale[None, :, None]
    w_main_all = jnp.concatenate([wt_all[:, :gate0], wt_all[:, b0:]], axis=1).astype(BF16)
    w_gate_all = jnp.pad(wt_all[:, gate0:b0], ((0, 0), (0, LANES - 2 * A_HEADS), (0, 0))).astype(BF16)
    w_out_all = w_out_ab.astype(BF16)
    w_in_c_all = w_in_c.astype(BF16)
    pool_w_all = pool_w.astype(BF16)
    w_out_c_all = w_out_c.astype(BF16)

    xf = x.reshape(n, d)
    for layer in range(depth):
        pre = pre_norm_g[layer][None, :]
        post = post_norm_g[layer][None, :]
        if layer % 2 == 0:
            e = layer // 2
            u, gc, gr = _in_proj(xf, pre, w_main_all[e], w_gate_all[e], conv_qk[e], seq, tm_in, 3 * a_width,
                                 2 * a_width, a_dh ** -0.5)

            bias = jnp.concatenate([bias_i[e], bias_f[e]])
            brow = jnp.pad(bias, (0, LANES - 2 * A_HEADS))[None, :]
            bcol = jnp.pad(bias, (0, GATE_ROWS - 2 * A_HEADS))[:, None]
            ya = _mlstm(u, gc, gr, brow, bcol, head_norm_g[e][None, :], tril, triu,
                        batch, seq, _pick(seq, 2 * chunk), A_HEADS, a_dh)
            yb = _stick_breaking(u, later, batch, seq, tq, B_HEADS, b_dh,
                                 sb0, sb0 + sb_step, sb0 + 2 * sb_step, sb0 + 3 * sb_step)
            xf = _out_proj(ya, yb, w_out_all[e], xf, post, tm_out)
        else:
            o = layer // 2
            xf = _odd_layer(xf, pre, post, w_in_c_all[o], pool_w_all[o],
                            pool_scale[o][None, :], w_out_c_all[o], seq, tm_odd)
    return xf.reshape(batch, seq, d)
```

```python
import functools

import jax
import jax.numpy as jnp
from jax import lax
from jax.experimental import pallas as pl
from jax.experimental.pallas import tpu as pltpu

F32 = jnp.float32
BF16 = jnp.bfloat16
EPS = 1e-6

A_HEADS = 4
B_HEADS = 8
CONV_WIDTH = 4
POOL_WINDOWS = (2, 4, 8, 16)
LANES = 128
GATE_ROWS = 16
POOL_HALO = 16
CONV_HALO = 8
SB_SUB = 128
SB_WIN = 256
SB_HEADS_PER_STEP = 2
SB_SKIP = 88.0
SB_DONE = 3.0e38
LOG2E = 1.4426950408889634
VMEM_LIMIT = 56 * 1024 * 1024

NT_DIMS = (((1,), (1,)), ((), ()))
TN_DIMS = (((0,), (0,)), ((), ()))


def _sigmoid(x):
    return 0.5 * jnp.tanh(0.5 * x) + 0.5


def _silu(x):
    half = 0.5 * x
    return half * jnp.tanh(half) + half


def _softplus(x):
    return jnp.maximum(x, 0.0) + jnp.log1p(jnp.exp(-jnp.abs(x)))


def _log_sigmoid(x):
    return -_softplus(-x)


def _split3(x):
    hi = x.astype(BF16)
    r1 = x - hi.astype(F32)
    mid = r1.astype(BF16)
    lo = (r1 - mid.astype(F32)).astype(BF16)
    return hi, mid, lo


def _dot(a, b):
    return jnp.dot(a, b, preferred_element_type=F32)


def _dot_nt(a, b):
    return lax.dot_general(a, b, NT_DIMS, preferred_element_type=F32)


def _in_proj_kernel(x_ref, g_ref, w_ref, wg_ref, cw_ref, u_ref, gc_ref, gr_ref, h_ref, hist_ref,
                    *, tiles_per_seq, conv_cols, k_scale):
    tm = x_ref.shape[0]

    @pl.when(pl.program_id(1) == 0)
    def _():
        @pl.when(pl.program_id(0) % tiles_per_seq == 0)
        def _():
            hist_ref[:, 0:CONV_HALO, :] = jnp.zeros((hist_ref.shape[0], CONV_HALO, LANES), F32)

        x = x_ref[...]
        ms = jnp.mean(x * x, axis=-1, keepdims=True)
        h = (x * lax.rsqrt(ms + EPS) * g_ref[...]).astype(BF16)
        h_ref[...] = h
        wg = wg_ref[...]
        gc_ref[...] = _dot_nt(h, wg)
        gr_ref[...] = _dot_nt(wg[:GATE_ROWS], h)

        pair = 2 * LANES

        def project(c2):
            return _dot_nt(h, w_ref[c2 * pair:(c2 + 1) * pair, :])

        def stash(c2):
            r = project(c2)
            for half in range(2):
                hist_ref[2 * c2 + half, CONV_HALO:, :] = r[:, half * LANES:(half + 1) * LANES]

        def conv(c2):
            for c in (2 * c2, 2 * c2 + 1):
                lanes = slice(c * LANES, (c + 1) * LANES)
                y = None
                for k in range(CONV_WIDTH):
                    term = (hist_ref[c, CONV_HALO - k:CONV_HALO - k + tm, :]
                            * cw_ref[CONV_WIDTH - 1 - k:CONV_WIDTH - k, lanes])
                    y = term if y is None else y + term
                hist_ref[c, 0:CONV_HALO, :] = hist_ref[c, tm:tm + CONV_HALO, :]
                y = _silu(y)
                if c * LANES >= conv_cols // 2:
                    y = y * k_scale
                u_ref[:, lanes] = y.astype(BF16)

        n_conv = conv_cols // pair
        stash(0)
        for c2 in range(1, n_conv):
            stash(c2)
            conv(c2 - 1)
        for c2 in range(n_conv, u_ref.shape[1] // pair):
            u_ref[:, c2 * pair:(c2 + 1) * pair] = project(c2).astype(BF16)
            if c2 == n_conv:
                conv(n_conv - 1)

    @pl.when(pl.program_id(1) > 0)
    def _():
        u_ref[...] = _dot_nt(h_ref[...], w_ref[...]).astype(BF16)


def _in_proj(x, g, w_main_t, w_gate_t, conv_w, seq, tm, tn, conv_cols, k_scale):
    n, d = x.shape
    cols = w_main_t.shape[0]
    assert conv_cols < tn and conv_cols % (2 * LANES) == 0 and tn % (2 * LANES) == 0
    return pl.pallas_call(
        functools.partial(_in_proj_kernel, tiles_per_seq=seq // tm, conv_cols=conv_cols, k_scale=k_scale),
        grid=(n // tm, cols // tn),
        in_specs=[
            pl.BlockSpec((tm, d), lambda i, j: (i, 0)),
            pl.BlockSpec((1, d), lambda i, j: (0, 0)),
            pl.BlockSpec((tn, d), lambda i, j: (j, 0)),
            pl.BlockSpec((LANES, d), lambda i, j: (0, 0)),
            pl.BlockSpec((CONV_WIDTH, conv_cols), lambda i, j: (0, 0)),
        ],
        out_specs=[
            pl.BlockSpec((tm, tn), lambda i, j: (i, j)),
            pl.BlockSpec((tm, LANES), lambda i, j: (i, 0)),
            pl.BlockSpec((GATE_ROWS, tm), lambda i, j: (0, i)),
        ],
        out_shape=[
            jax.ShapeDtypeStruct((n, cols), BF16),
            jax.ShapeDtypeStruct((n, LANES), F32),
            jax.ShapeDtypeStruct((GATE_ROWS, n), F32),
        ],
        scratch_shapes=[pltpu.VMEM((tm, d), BF16),
                        pltpu.VMEM((conv_cols // LANES, CONV_HALO + tm, LANES), F32)],
        compiler_params=pltpu.CompilerParams(
            dimension_semantics=("arbitrary", "arbitrary"), vmem_limit_bytes=VMEM_LIMIT),
        name="even_in_proj",
    )(x, g, w_main_t, w_gate_t, conv_w)


def _mlstm_block(rows, q_ref, k_ref, v_ref, o_ref, z_ref, gc_ref, gr_ref, brow_ref, bcol_ref, hng_ref,
                 tril, triu, y_ref, cts, m_prevs, *, heads, dh):
    chunk = rows.stop - rows.start
    gact_c = gc_ref[rows, :] + brow_ref[...]
    gact_r = gr_ref[:, rows] + bcol_ref[...]
    b_c_all = sum(_dot(tril, p) for p in _split3(_log_sigmoid(gact_c)))
    b_r_all = sum(_dot(p, triu) for p in _split3(_log_sigmoid(gact_r)))

    t_idx = lax.broadcasted_iota(jnp.int32, (chunk, chunk), 0)
    s_idx = lax.broadcasted_iota(jnp.int32, (chunk, chunk), 1)
    causal = s_idx <= t_idx
    ones_blk = jnp.where(lax.broadcasted_iota(jnp.int32, (chunk, LANES), 1) == 0, 1.0, 0.0).astype(BF16)

    hs = range(heads)
    sls = [slice(h * dh, (h + 1) * dh) for h in hs]
    qbs = [q_ref[rows, sl] for sl in sls]
    kbs = [k_ref[rows, sl] for sl in sls]
    v_augs = [jnp.concatenate([v_ref[rows, sl], ones_blk], axis=1) for sl in sls]
    b_cs = [b_c_all[:, heads + h:heads + h + 1] for h in hs]

    raw_scores = [_dot_nt(qb, kb) for qb, kb in zip(qbs, kbs)]
    inters = [_dot(qb, ct.astype(BF16)) for qb, ct in zip(qbs, cts)]

    scores, m_ts, s_inters = [], [], []
    for h in hs:
        b_r = b_r_all[heads + h:heads + h + 1, :]
        ig_r = gact_r[h:h + 1, :]
        dmat = jnp.where(causal, b_cs[h] + (ig_r - b_r), -1e30)
        m_inter = b_cs[h] + m_prevs[h]
        m_t = jnp.maximum(m_inter, jnp.max(dmat, axis=1, keepdims=True))
        scores.append((raw_scores[h] * jnp.exp(dmat - m_t)).astype(BF16))
        m_ts.append(m_t)
        s_inters.append(jnp.exp(m_inter - m_t))

    nums = [_dot(scores[h], v_augs[h]) + s_inters[h] * inters[h] for h in hs]

    new_cts, new_ms = [], []
    for h in hs:
        g_end = b_cs[h][chunk - 1:chunk, :]
        a_c = g_end - b_cs[h] + gact_c[:, h:h + 1]
        m_new = jnp.maximum(g_end + m_prevs[h], jnp.max(a_c, axis=0, keepdims=True))
        decay = jnp.exp(g_end + m_prevs[h] - m_new)
        kw = (kbs[h].astype(F32) * jnp.exp(a_c - m_new)).astype(BF16)
        new_cts.append(decay * cts[h] + lax.dot_general(kw, v_augs[h], TN_DIMS, preferred_element_type=F32))
        new_ms.append(m_new)

    for h in hs:
        sl = sls[h]
        den = nums[h][:, dh:dh + 1]
        hh = nums[h][:, :dh] / jnp.maximum(jnp.abs(den), jnp.exp(-m_ts[h]))
        ha = _sigmoid(o_ref[rows, sl].astype(F32)) * hh
        ha = ha * lax.rsqrt(jnp.mean(ha * ha, axis=-1, keepdims=True) + EPS)
        y_ref[rows, sl] = (ha * hng_ref[:, sl] * _silu(z_ref[rows, sl].astype(F32))).astype(BF16)
    return new_cts, new_ms


def _mlstm_kernel(q_ref, k_ref, v_ref, o_ref, z_ref, gc_ref, gr_ref, brow_ref, bcol_ref,
                  hng_ref, tril_ref, triu_ref, y_ref, ct_ref, m_ref, *, heads, dh):
    chunk = tril_ref.shape[0]

    @pl.when(pl.program_id(1) == 0)
    def _():
        ct_ref[...] = jnp.zeros_like(ct_ref)
        m_ref[...] = jnp.zeros_like(m_ref)

    tril = tril_ref[...]
    triu = triu_ref[...]
    cts = [ct_ref[h] for h in range(heads)]
    ms = [m_ref[h:h + 1, 0:1] for h in range(heads)]
    for sub in range(q_ref.shape[0] // chunk):
        rows = slice(sub * chunk, (sub + 1) * chunk)
        cts, ms = _mlstm_block(rows, q_ref, k_ref, v_ref, o_ref, z_ref, gc_ref, gr_ref, brow_ref, bcol_ref,
                               hng_ref, tril, triu, y_ref, cts, ms, heads=heads, dh=dh)
    for h in range(heads):
        ct_ref[h] = cts[h]
        m_ref[h:h + 1, :] = jnp.broadcast_to(ms[h], (1, LANES))


def _mlstm(u, gc, gr, brow, bcol, hn_g, tril, triu, batch, seq, chunk, heads, dh):
    n = batch * seq
    nc = seq // chunk
    width = heads * dh
    blk = tril.shape[0]

    def col_block(c):
        return pl.BlockSpec((chunk, width), lambda b, i, c=c: (b * nc + i, c))

    def fixed(shape):
        return pl.BlockSpec(shape, lambda b, i: (0,) * len(shape))

    return pl.pallas_call(
        functools.partial(_mlstm_kernel, heads=heads, dh=dh),
        grid=(batch, nc),
        in_specs=[
            col_block(0), col_block(1), col_block(2), col_block(3), col_block(4),
            pl.BlockSpec((chunk, LANES), lambda b, i: (b * nc + i, 0)),
            pl.BlockSpec((GATE_ROWS, chunk), lambda b, i: (0, b * nc + i)),
            fixed((1, LANES)), fixed((GATE_ROWS, 1)),
            fixed((1, width)), fixed((blk, blk)), fixed((blk, blk)),
        ],
        out_specs=pl.BlockSpec((chunk, width), lambda b, i: (b * nc + i, 0)),
        out_shape=jax.ShapeDtypeStruct((n, width), BF16),
        scratch_shapes=[
            pltpu.VMEM((heads, dh, dh + LANES), F32),
            pltpu.VMEM((8, LANES), F32),
        ],
        compiler_params=pltpu.CompilerParams(
            dimension_semantics=("arbitrary", "arbitrary"), vmem_limit_bytes=VMEM_LIMIT),
        name="even_mlstm",
    )(u, u, u, u, u, gc, gr, brow, bcol, hn_g, tril, triu)


def _sb_windows(jobs, k_ref, v_ref, later):
    zs = [_dot_nt(q, k_ref[pl.ds(start, SB_WIN), lanes]) for q, lanes, start, _, _ in jobs]
    sps = []
    for z, (_, _, _, keep, _) in zip(zs, jobs):
        softplus = jnp.maximum(z, 0.0) + jnp.log(1.0 + jnp.exp2(jnp.abs(z) * -LOG2E))
        sps.append(jnp.where(keep, softplus, 0.0))
    afters = [_dot(sp.astype(BF16), later) for sp in sps]
    results = []
    for z, sp, after, (_, lanes, start, keep, acc) in zip(zs, sps, afters, jobs):
        total = after[:, 0:1] + sp[:, 0:1]
        expo = (z - sp) - after if acc is None else (z - sp) - (after + acc)
        a = jnp.where(keep, jnp.exp(expo), 0.0)
        results.append((_dot(a.astype(BF16), v_ref[pl.ds(start, SB_WIN), lanes]), total))
    return results


def _sb_kernel(q_ref, k_ref, v_ref, z_ref, m_ref, y_ref, out_ref, acc_ref, *, dh):
    tq, width = q_ref.shape
    i = pl.program_id(2)
    later = m_ref[...]
    col = lax.broadcasted_iota(jnp.int32, (SB_SUB, SB_WIN), 1)
    col_minus_row = col - lax.broadcasted_iota(jnp.int32, (SB_SUB, SB_WIN), 0)
    tiles = [(sub, hh) for sub in range(tq // SB_SUB) for hh in range(width // dh)]

    def span(sub, hh):
        return slice(sub * SB_SUB, (sub + 1) * SB_SUB), slice(hh * dh, (hh + 1) * dh)

    jobs = []
    for sub, hh in tiles:
        rows, lanes = span(sub, hh)
        q0 = i * tq + sub * SB_SUB
        start = pl.multiple_of(jnp.maximum(q0 - (SB_WIN - SB_SUB), 0), SB_SUB)
        jobs.append((q_ref[rows, lanes], lanes, start, col_minus_row < (q0 - start), None))
    pending = []
    for n, ((sub, hh), (_, _, start, _, _), (out, total)) in enumerate(
            zip(tiles, jobs, _sb_windows(jobs, k_ref, v_ref, later))):
        rows, lanes = span(sub, hh)
        out_ref[rows, lanes] = out
        acc_ref[n] = total
        pending.append(jnp.where(start > 0, total, SB_DONE))
    low = jnp.min(functools.reduce(jnp.minimum, pending))

    def more(carry):
        _, low = carry
        return low < SB_SKIP

    def one_round(carry):
        rnd, _ = carry
        jobs = []
        for n, (sub, hh) in enumerate(tiles):
            rows, lanes = span(sub, hh)
            limit = i * tq + sub * SB_SUB - (SB_WIN - SB_SUB) - rnd * SB_WIN
            start = pl.multiple_of(jnp.maximum(limit - SB_WIN, 0), SB_SUB)
            jobs.append((q_ref[rows, lanes], lanes, start, col < (limit - start), acc_ref[n]))
        pending = []
        for n, ((sub, hh), (_, _, start, _, acc), (out, total)) in enumerate(
                zip(tiles, jobs, _sb_windows(jobs, k_ref, v_ref, later))):
            rows, lanes = span(sub, hh)
            out_ref[rows, lanes] += out
            acc = acc + total
            acc_ref[n] = acc
            pending.append(jnp.where(start > 0, acc, SB_DONE))
        return rnd + 1, jnp.min(functools.reduce(jnp.minimum, pending))

    lax.while_loop(more, one_round, (0, low))
    y_ref[...] = (out_ref[...] * _silu(z_ref[...].astype(F32))).astype(BF16)


def _stick_breaking(u, later, batch, seq, tq, heads, dh, q_col, k_col, v_col, z_col):
    n = batch * seq
    nq = seq // tq
    width = SB_HEADS_PER_STEP * dh
    tiles = (tq // SB_SUB) * SB_HEADS_PER_STEP
    return pl.pallas_call(
        functools.partial(_sb_kernel, dh=dh),
        grid=(batch, heads // SB_HEADS_PER_STEP, nq),
        in_specs=[
            pl.BlockSpec((tq, width), lambda b, h, i: (b * nq + i, q_col + h)),
            pl.BlockSpec((seq, width), lambda b, h, i: (b, k_col + h)),
            pl.BlockSpec((seq, width), lambda b, h, i: (b, v_col + h)),
            pl.BlockSpec((tq, width), lambda b, h, i: (b * nq + i, z_col + h)),
            pl.BlockSpec((SB_WIN, SB_WIN), lambda b, h, i: (0, 0)),
        ],
        out_specs=pl.BlockSpec((tq, width), lambda b, h, i: (b * nq + i, h)),
        out_shape=jax.ShapeDtypeStruct((n, heads * dh), BF16),
        scratch_shapes=[pltpu.VMEM((tq, width), F32), pltpu.VMEM((tiles, SB_SUB, 1), F32)],
        compiler_params=pltpu.CompilerParams(
            dimension_semantics=("arbitrary", "arbitrary", "arbitrary"), vmem_limit_bytes=VMEM_LIMIT),
        name="even_stick_breaking",
    )(u, u, u, u, later)


def _out_proj_kernel(ya_ref, yb_ref, wa_ref, wb_ref, x_ref, g_ref, o_ref):
    y = _dot(ya_ref[...], wa_ref[...]) + _dot(yb_ref[...], wb_ref[...])
    ms = jnp.mean(y * y, axis=-1, keepdims=True)
    o_ref[...] = x_ref[...] + y * lax.rsqrt(ms + EPS) * g_ref[...]


def _out_proj(ya, yb, w, x, g, tm):
    n, d = x.shape
    ka = ya.shape[1]
    kb = yb.shape[1]
    assert ka == kb and w.shape == (ka + kb, d)
    return pl.pallas_call(
        _out_proj_kernel,
        grid=(n // tm,),
        in_specs=[
            pl.BlockSpec((tm, ka), lambda i: (i, 0)),
            pl.BlockSpec((tm, kb), lambda i: (i, 0)),
            pl.BlockSpec((ka, d), lambda i: (0, 0)),
            pl.BlockSpec((kb, d), lambda i: (1, 0)),
            pl.BlockSpec((tm, d), lambda i: (i, 0)),
            pl.BlockSpec((1, d), lambda i: (0, 0)),
        ],
        out_specs=pl.BlockSpec((tm, d), lambda i: (i, 0)),
        out_shape=jax.ShapeDtypeStruct((n, d), F32),
        compiler_params=pltpu.CompilerParams(
            dimension_semantics=("arbitrary",), vmem_limit_bytes=VMEM_LIMIT),
        name="even_out_proj",
    )(ya, yb, w, w, x, g)


def _odd_kernel(x_ref, pre_ref, post_ref, wp_ref, wz_ref, pw_ref, ps_ref, wo_ref, o_ref, tail_ref,
                *, tiles_per_seq):
    tm = x_ref.shape[0]
    gdim = pw_ref.shape[1]
    tile = pl.program_id(0) % tiles_per_seq

    @pl.when(tile == 0)
    def _():
        tail_ref[...] = jnp.zeros_like(tail_ref)

    x = x_ref[...]
    ms = jnp.mean(x * x, axis=-1, keepdims=True)
    h = (x * lax.rsqrt(ms + EPS) * pre_ref[...]).astype(BF16)
    pos = tile * tm + lax.broadcasted_iota(jnp.int32, (tm, 1), 0)

    y = jnp.zeros((tm, o_ref.shape[1]), F32)
    for g, window in enumerate(POOL_WINDOWS):
        sl = slice(g * gdim, (g + 1) * gdim)
        p = _dot(h, wp_ref[:, sl])
        zg = _dot(h, wz_ref[:, sl])
        ext = jnp.concatenate([tail_ref[:, sl], p], axis=0)
        tail_ref[:, sl] = p[tm - POOL_HALO:, :]
        span = 1
        while span < window:
            ext = ext + pltpu.roll(ext, span, axis=0)
            span *= 2
        count = jnp.minimum(pos + 1, window).astype(F32)
        pooled = ext[POOL_HALO:, :] / count - p
        mixed = _dot(pooled.astype(BF16), pw_ref[g]) * ps_ref[:, sl]
        y = y + _dot((mixed * _silu(zg)).astype(BF16), wo_ref[sl, :])

    ms = jnp.mean(y * y, axis=-1, keepdims=True)
    o_ref[...] = x + y * lax.rsqrt(ms + EPS) * post_ref[...]


def _odd_layer(x, pre_g, post_g, w_in, pool_w, pool_scale, w_out, seq, tm):
    n, d = x.shape
    cw = w_in.shape[1] // 2
    groups, gdim, _ = pool_w.shape

    def fixed(shape, index=None):
        index = index or (0,) * len(shape)
        return pl.BlockSpec(shape, lambda i: index, pipeline_mode=pl.Buffered(1))

    return pl.pallas_call(
        functools.partial(_odd_kernel, tiles_per_seq=seq // tm),
        grid=(n // tm,),
        in_specs=[
            pl.BlockSpec((tm, d), lambda i: (i, 0)),
            fixed((1, d)), fixed((1, d)), fixed((d, cw)), fixed((d, cw), (0, 1)),
            fixed((groups, gdim, gdim)), fixed((1, cw)), fixed((cw, d)),
        ],
        out_specs=pl.BlockSpec((tm, d), lambda i: (i, 0)),
        out_shape=jax.ShapeDtypeStruct((n, d), F32),
        scratch_shapes=[pltpu.VMEM((POOL_HALO, cw), F32)],
        compiler_params=pltpu.CompilerParams(
            dimension_semantics=("arbitrary",), vmem_limit_bytes=VMEM_LIMIT),
        name="odd_layer",
    )(x, pre_g, post_g, w_in, w_in, pool_w, pool_scale, w_out)


def _pick(total, prefer):
    t = min(total, prefer)
    assert total % t == 0, (total, prefer)
    return t


def kernel(x, pre_norm_g, post_norm_g, w_in_ab, conv_qk, bias_i, bias_f, head_norm_g, w_out_ab,
           w_in_c, pool_w, pool_scale, w_out_c):
    batch, seq, d = x.shape
    depth = pre_norm_g.shape[0]
    a_width = head_norm_g.shape[1]
    a_dh = a_width // A_HEADS
    b_width = w_out_ab.shape[1] - a_width
    b_dh = b_width // B_HEADS
    n = batch * seq

    chunk = _pick(seq, 256)
    tq = _pick(seq, 2048)
    tm_in = _pick(seq, 1024)
    tm_out = _pick(n, 1024)
    tm_odd = _pick(seq, 1024)

    idx = jnp.arange(chunk)
    tril = (idx[None, :] <= idx[:, None]).astype(BF16)
    triu = (idx[:, None] <= idx[None, :]).astype(BF16)
    idw = jnp.arange(SB_WIN)
    later = (idw[:, None] > idw[None, :]).astype(BF16)

    gate0 = 5 * a_width
    b0 = gate0 + 2 * A_HEADS
    sb_cols = SB_HEADS_PER_STEP * b_dh
    sb0 = gate0 // sb_cols
    sb_step = b_width // sb_cols

    row_scale = jnp.ones((w_in_ab.shape[2],), F32).at[b0:b0 + b_width].set(b_dh ** -0.5)
    wt_all = jnp.swapaxes(w_in_ab, 1, 2) * row_scale[None, :, None]
    w_main_all = jnp.concatenate([wt_all[:, :gate0], wt_all[:, b0:]], axis=1).astype(BF16)
    w_gate_all = jnp.pad(wt_all[:, gate0:b0], ((0, 0), (0, LANES - 2 * A_HEADS), (0, 0))).astype(BF16)
    w_out_all = w_out_ab.astype(BF16)
    w_in_c_all = w_in_c.astype(BF16)
    pool_w_all = pool_w.astype(BF16)
    w_out_c_all = w_out_c.astype(BF16)

    xf = x.reshape(n, d)
    for layer in range(depth):
        pre = pre_norm_g[layer][None, :]
        post = post_norm_g[layer][None, :]
        if layer % 2 == 0:
            e = layer // 2
            u, gc, gr = _in_proj(xf, pre, w_main_all[e], w_gate_all[e], conv_qk[e], seq, tm_in, 3 * a_width,
                                 2 * a_width, a_dh ** -0.5)

            bias = jnp.concatenate([bias_i[e], bias_f[e]])
            brow = jnp.pad(bias, (0, LANES - 2 * A_HEADS))[None, :]
            bcol = jnp.pad(bias, (0, GATE_ROWS - 2 * A_HEADS))[:, None]
            ya = _mlstm(u, gc, gr, brow, bcol, head_norm_g[e][None, :], tril, triu,
                        batch, seq, _pick(seq, 4 * chunk), A_HEADS, a_dh)
            yb = _stick_breaking(u, later, batch, seq, tq, B_HEADS, b_dh,
                                 sb0, sb0 + sb_step, sb0 + 2 * sb_step, sb0 + 3 * sb_step)
            xf = _out_proj(ya, yb, w_out_all[e], xf, post, tm_out)
        else:
            o = layer // 2
            xf = _odd_layer(xf, pre, post, w_in_c_all[o], pool_w_all[o],
                            pool_scale[o][None, :], w_out_c_all[o], seq, tm_odd)
    return xf.reshape(batch, seq, d)
```
